```python
import math
import jax, jax.numpy as jnp
from jax import lax
import numpy as np

D_MODEL = 1024
BATCH = 8
SEQ = 4096
DEPTH = 1
DEC_BATCH = 16
DEC_SEQ = 2048
PAST_LEN = 128

HEAD_DIM = 64
N_HEADS_A = 8
N_HEADS_B = 4
N_HEADS_TOTAL = N_HEADS_A + N_HEADS_B
DILATED_BRANCHES = ((128, 1), (512, 4), (2048, 16))
A_WIDTH = N_HEADS_A * HEAD_DIM
B_QK_WIDTH = N_HEADS_B * 2 * HEAD_DIM
B_V_WIDTH = N_HEADS_B * 2 * HEAD_DIM
IN_WIDTH = 3 * A_WIDTH + 2 * B_QK_WIDTH + B_V_WIDTH
MIX_WIDTH = A_WIDTH + B_V_WIDTH
NUM_BUCKETS = 32
MAX_DISTANCE = 1024
Q_BLOCK = 128
N_KEYS = 128
N_EXPERTS = N_KEYS * N_KEYS
PEER_HEADS = 8
D_KEY = 256
D_KEY_HALF = D_KEY // 2
PEER_TOPK = 16
TOKEN_CHUNK = 128
NORM_EPS = 1e-6
NEG_INF = -1e30
ATTN_SCALE = HEAD_DIM ** -0.5

kernel_name = 'hymba_dilated_diff_peer_encoder'


def rms_norm(x, g):
    x32 = x.astype(jnp.float32)
    y = x32 * lax.rsqrt(jnp.mean(x32 * x32, axis=-1, keepdims=True) + NORM_EPS)
    return (y * g.astype(jnp.float32)).astype(x.dtype)


def rel_bucket(rel):
    nb = NUM_BUCKETS // 2
    max_exact = nb // 2
    n = jnp.abs(rel)
    large = max_exact + (jnp.log(jnp.maximum(n, 1).astype(jnp.float32) / max_exact)
                         / math.log(MAX_DISTANCE / max_exact) * (nb - max_exact)).astype(jnp.int32)
    large = jnp.minimum(large, nb - 1)
    return (rel > 0).astype(jnp.int32) * nb + jnp.where(n < max_exact, n, large)


def dilated_branch(q, k, v, bias_tab, dilation, half):
    b, s, h, c = q.shape
    blk = half
    n_len = s // dilation
    nb = -(-n_len // blk)
    pad = nb * blk - n_len

    def to_classes(t):
        return t.reshape(b, n_len, dilation, h, t.shape[-1]).transpose(0, 2, 1, 3, 4)

    qc = jnp.pad(to_classes(q), ((0, 0), (0, 0), (0, pad), (0, 0), (0, 0))).reshape(b, dilation, nb, blk, h, c)

    def band(t):
        tp = jnp.pad(to_classes(t), ((0, 0), (0, 0), (blk, blk + pad), (0, 0), (0, 0)))
        tp = tp.reshape(b, dilation, nb + 2, blk, h, t.shape[-1])
        return jnp.concatenate([tp[:, :, :-2], tp[:, :, 1:-1], tp[:, :, 2:]], axis=3)

    kb, vb = band(k), band(v)
    off = jnp.arange(3 * blk)[None, :] - blk - jnp.arange(blk)[:, None]
    key_idx = jnp.arange(nb)[:, None] * blk - blk + jnp.arange(3 * blk)[None, :]
    mask = (jnp.abs(off) <= half)[None] & ((key_idx >= 0) & (key_idx < n_len))[:, None, :]
    bias = bias_tab[rel_bucket(off * dilation)].transpose(2, 0, 1).astype(jnp.float32)
    scores = jnp.einsum('bdnqhc,bdnkhc->bdnhqk', qc, kb).astype(jnp.float32) * ATTN_SCALE + bias
    scores = jnp.where(mask[:, None], scores, NEG_INF)
    lse = jax.nn.logsumexp(scores, axis=-1)
    p = jnp.exp(scores - lse[..., None])
    o = jnp.einsum('bdnhqk,bdnkhc->bdnqhc', p.astype(v.dtype), vb)

    def from_classes(t):
        t = t.reshape(b, dilation, nb * blk, h, t.shape[-1])[:, :, :n_len]
        return t.transpose(0, 2, 1, 3, 4).reshape(b, s, h, t.shape[-1])

    o = from_classes(o)
    lse_t = from_classes(lse.transpose(0, 1, 2, 4, 3)[..., None])[..., 0]
    return o, lse_t


def dilated_attention(q, k, v, bias_tab):
    outs, lses = [], []
    for window, dilation in DILATED_BRANCHES:
        o, l = dilated_branch(q, k, v, bias_tab, dilation, window // dilation // 2)
        outs.append(o)
        lses.append(l)
    w = jax.nn.softmax(jnp.stack(lses, 0), axis=0)
    return jnp.sum(w[..., None] * jnp.stack(outs, 0).astype(jnp.float32), axis=0)


def diff_attention(q1, q2, k1, k2, v, bias_tab, lam):
    b, s, h, c = q1.shape
    nq = s // Q_BLOCK
    qs = jnp.stack([q1, q2], 0).reshape(2, b, nq, Q_BLOCK, h, c).transpose(2, 0, 1, 3, 4, 5)
    kpos = jnp.arange(s)

    def block(args):
        qblk, start = args
        qpos = start + jnp.arange(Q_BLOCK)
        bias = bias_tab[rel_bucket(kpos[None, :] - qpos[:, None])].transpose(2, 0, 1).astype(jnp.float32)
        s1 = jnp.einsum('bqhc,bkhc->bhqk', qblk[0], k1).astype(jnp.float32) * ATTN_SCALE + bias
        s2 = jnp.einsum('bqhc,bkhc->bhqk', qblk[1], k2).astype(jnp.float32) * ATTN_SCALE + bias
        a = jax.nn.softmax(s1, axis=-1) - lam * jax.nn.softmax(s2, axis=-1)
        return jnp.einsum('bhqk,bkhc->bqhc', a.astype(v.dtype), v)

    o = lax.map(block, (qs, jnp.arange(nq) * Q_BLOCK))
    return o.transpose(1, 0, 2, 3, 4).reshape(b, s, h, v.shape[-1])


def peer(h, w_q, sub_keys, u, v):
    b, s, d = h.shape
    hc = h.reshape(-1, TOKEN_CHUNK, d)

    def chunk(xc):
        q = (xc @ w_q).reshape(TOKEN_CHUNK, PEER_HEADS, 2, D_KEY_HALF)
        sc = jnp.einsum('cpzk,znk->cpzn', q, sub_keys).astype(jnp.float32)
        top_v, top_i = lax.top_k(sc, PEER_TOPK)
        cand = (top_v[:, :, 0, :, None] + top_v[:, :, 1, None, :]).reshape(TOKEN_CHUNK, PEER_HEADS, -1)
        cand_idx = (top_i[:, :, 0, :, None] * N_KEYS + top_i[:, :, 1, None, :]).reshape(TOKEN_CHUNK, PEER_HEADS, -1)
        best, pos = lax.top_k(cand, PEER_TOPK)
        eidx = jnp.take_along_axis(cand_idx, pos, axis=-1)
        g = jax.nn.softmax(best, axis=-1)
        act = jax.nn.gelu(jnp.einsum('cd,cpkd->cpk', xc, u[eidx]).astype(jnp.float32), approximate=False)
        return jnp.einsum('cpk,cpkd->cd', (g * act).astype(xc.dtype), v[eidx])

    return lax.map(chunk, hc).reshape(b, s, d)


def encoder(x, rel_bias, attn_norm_g, w_in, q_norm_a, k_norm_a, q_norm_b, k_norm_b,
            lambda_q1, lambda_k1, lambda_q2, lambda_k2, diff_norm_g, w_out, ffn_norm_g,
            peer_w_q, peer_sub_keys, peer_u, peer_v):
    b, s, _ = x.shape
    bias_a = rel_bias[:, :N_HEADS_A]
    bias_b = rel_bias[:, N_HEADS_A:]
    for l in range(DEPTH):
        lambda_init = 0.8 - 0.6 * math.exp(-0.3 * l)
        n = rms_norm(x, attn_norm_g[l])
        proj = n @ w_in[l]
        qa, ka, va, qb, kb, vb = jnp.split(proj, np.cumsum(
            [A_WIDTH, A_WIDTH, A_WIDTH, B_QK_WIDTH, B_QK_WIDTH]).tolist(), axis=-1)
        qa = rms_norm(qa.reshape(b, s, N_HEADS_A, HEAD_DIM), q_norm_a[l])
        ka = rms_norm(ka.reshape(b, s, N_HEADS_A, HEAD_DIM), k_norm_a[l])
        va = va.reshape(b, s, N_HEADS_A, HEAD_DIM)
        o_a = dilated_attention(qa, ka, va, bias_a).reshape(b, s, A_WIDTH).astype(x.dtype)

        qb = rms_norm(qb.reshape(b, s, N_HEADS_B, 2, HEAD_DIM), q_norm_b[l])
        kb = rms_norm(kb.reshape(b, s, N_HEADS_B, 2, HEAD_DIM), k_norm_b[l])
        vb = vb.reshape(b, s, N_HEADS_B, 2 * HEAD_DIM)
        lam = (jnp.exp(jnp.sum(lambda_q1[l].astype(jnp.float32) * lambda_k1[l].astype(jnp.float32)))
               - jnp.exp(jnp.sum(lambda_q2[l].astype(jnp.float32) * lambda_k2[l].astype(jnp.float32)))
               + lambda_init)
        o_b = diff_attention(qb[..., 0, :], qb[..., 1, :], kb[..., 0, :], kb[..., 1, :], vb, bias_b, lam)
        o_b = (rms_norm(o_b, diff_norm_g[l]) * (1.0 - lambda_init)).reshape(b, s, B_V_WIDTH).astype(x.dtype)

        x = x + jnp.concatenate([o_a, o_b], axis=-1) @ w_out[l]
        x = x + peer(rms_norm(x, ffn_norm_g[l]), peer_w_q[l], peer_sub_keys[l], peer_u[l], peer_v[l])
    return x


def setup_inputs(seed: int = 0) -> dict:
    key = jax.random.key(seed)
    ks = jax.random.split(key, 20)
    f32 = jnp.float32

    def gain(k, shape):
        return 1.0 + 0.01 * jax.random.normal(k, shape, f32)

    return {
        'x_prompt': jax.random.normal(ks[0], (BATCH, SEQ, D_MODEL), f32),
        'x_sample': jax.random.normal(ks[1], (DEC_BATCH, DEC_SEQ, D_MODEL), f32),
        'rel_bias': 0.1 * jax.random.normal(ks[2], (NUM_BUCKETS, N_HEADS_TOTAL), f32),
        'attn_norm_g': gain(ks[3], (DEPTH, D_MODEL)),
        'w_in': jax.random.normal(ks[4], (DEPTH, D_MODEL, IN_WIDTH), f32) * D_MODEL ** -0.5,
        'q_norm_a': gain(ks[5], (DEPTH, HEAD_DIM)),
        'k_norm_a': gain(ks[6], (DEPTH, HEAD_DIM)),
        'q_norm_b': gain(ks[7], (DEPTH, HEAD_DIM)),
        'k_norm_b': gain(ks[8], (DEPTH, HEAD_DIM)),
        'lambda_q1': 0.1 * jax.random.normal(ks[9], (DEPTH, HEAD_DIM), f32),
        'lambda_k1': 0.1 * jax.random.normal(ks[10], (DEPTH, HEAD_DIM), f32),
        'lambda_q2': 0.1 * jax.random.normal(ks[11], (DEPTH, HEAD_DIM), f32),
        'lambda_k2': 0.1 * jax.random.normal(ks[12], (DEPTH, HEAD_DIM), f32),
        'diff_norm_g': gain(ks[13], (DEPTH, 2 * HEAD_DIM)),
        'w_out': jax.random.normal(ks[14], (DEPTH, MIX_WIDTH, D_MODEL), f32) * MIX_WIDTH ** -0.5,
        'ffn_norm_g': gain(ks[15], (DEPTH, D_MODEL)),
        'peer_w_q': jax.random.normal(ks[16], (DEPTH, D_MODEL, PEER_HEADS * D_KEY), f32) * D_MODEL ** -0.5,
        'peer_sub_keys': jax.random.normal(ks[17], (DEPTH, 2, N_KEYS, D_KEY_HALF), f32) * D_KEY_HALF ** -0.5,
        'peer_u': jax.random.normal(ks[18], (DEPTH, N_EXPERTS, D_MODEL), f32) * D_MODEL ** -0.5,
        'peer_v': jax.random.normal(ks[19], (DEPTH, N_EXPERTS, D_MODEL), f32) * D_MODEL ** -0.5,
    }


def reference(x_prompt, x_sample, rel_bias, attn_norm_g, w_in, q_norm_a, k_norm_a, q_norm_b, k_norm_b,
              lambda_q1, lambda_k1, lambda_q2, lambda_k2, diff_norm_g, w_out, ffn_norm_g,
              peer_w_q, peer_sub_keys, peer_u, peer_v):
    y_prompt = encoder(x_prompt, rel_bias, attn_norm_g, w_in, q_norm_a, k_norm_a, q_norm_b, k_norm_b,
                       lambda_q1, lambda_k1, lambda_q2, lambda_k2, diff_norm_g, w_out, ffn_norm_g,
                       peer_w_q, peer_sub_keys, peer_u, peer_v)
    y_sample = encoder(x_sample, rel_bias, attn_norm_g, w_in, q_norm_a, k_norm_a, q_norm_b, k_norm_b,
                       lambda_q1, lambda_k1, lambda_q2, lambda_k2, diff_norm_g, w_out, ffn_norm_g,
                       peer_w_q, peer_sub_keys, peer_u, peer_v)
    return (y_prompt, y_sample)
```

```python
import functools
import math

import numpy as np
import jax
import jax.numpy as jnp
from jax import lax
from jax.experimental import pallas as pl
from jax.experimental.pallas import tpu as pltpu

F32 = jnp.float32
BF16 = jnp.bfloat16

D_MODEL = 1024
HEAD_DIM = 64
N_HEADS_A = 8
N_HEADS_B = 4
SEG = 512
IN_WIDTH = 6 * SEG
LANES = 128
DILATIONS = (1, 4, 16)
BAND_HALF = 64
NUM_BUCKETS = 32
MAX_DISTANCE = 1024
N_KEYS = 128
PEER_HEADS = 8
PEER_TOPK = 16
PICKS = PEER_HEADS * PEER_TOPK
NORM_EPS = 1e-6
NEG_INF = -1e30
ATTN_SCALE = HEAD_DIM ** -0.5
LAMBDA_INIT = 0.8 - 0.6 * math.exp(-0.3 * 0)

VMEM_LIMIT = 48 * 1024 * 1024


def _dot_nt(a, b):
    return lax.dot_general(a, b, (((1,), (1,)), ((), ())), preferred_element_type=F32)


def _in_proj_kernel(x_ref, g_ref, w_ref, gsum_ref, gains_ref, o_ref):
    x = x_ref[...]
    ms = jnp.mean(x * x, axis=-1, keepdims=True)
    xn = ((x * lax.rsqrt(ms + NORM_EPS)) * g_ref[...]).astype(BF16)
    gi = 0
    for seg in range(6):
        acc = jnp.dot(xn, w_ref[:, seg * SEG:(seg + 1) * SEG], preferred_element_type=F32)
        if seg in (0, 1, 3, 4):
            sq = acc * acc
            hi = sq.astype(BF16)
            lo = (sq - hi.astype(F32)).astype(BF16)
            gs = (jnp.dot(hi, gsum_ref[...], preferred_element_type=F32)
                  + jnp.dot(lo, gsum_ref[...], preferred_element_type=F32))
            acc = (acc * lax.rsqrt(gs * (1.0 / HEAD_DIM) + NORM_EPS)) * gains_ref[gi]
            gi += 1
        o_ref[:, seg * SEG:(seg + 1) * SEG] = acc.astype(BF16)


def _in_proj(x, g, w_bf, gsum, gains, tm=512):
    t = x.shape[0]
    return pl.pallas_call(
        _in_proj_kernel,
        out_shape=jax.ShapeDtypeStruct((t, IN_WIDTH), BF16),
        grid=(t // tm,),
        in_specs=[
            pl.BlockSpec((tm, D_MODEL), lambda i: (i, 0)),
            pl.BlockSpec((1, D_MODEL), lambda i: (0, 0)),
            pl.BlockSpec((D_MODEL, IN_WIDTH), lambda i: (0, 0)),
            pl.BlockSpec((SEG, SEG), lambda i: (0, 0)),
            pl.BlockSpec((4, 1, SEG), lambda i: (0, 0, 0)),
        ],
        out_specs=pl.BlockSpec((tm, IN_WIDTH), lambda i: (i, 0)),
        compiler_params=pltpu.CompilerParams(
            dimension_semantics=("parallel",), vmem_limit_bytes=VMEM_LIMIT),
        name="in_proj",
    )(x, g, w_bf, gsum, gains)


_QT = 128
_KT = _QT + 2 * BAND_HALF
_PAD = BAND_HALF * max(DILATIONS)


def _dilated_kernel(q_ref, k_ref, v_ref, bias_ref, o_ref,
                    qf, kf, vf, qc, kc, vc, ob, lb, *, seq):
    s = seq
    lane = lax.broadcasted_iota(jnp.int32, (1, LANES), 1)
    head0 = lane < HEAD_DIM

    qf[...] = q_ref[...].astype(F32)
    zpad = jnp.zeros((_PAD, LANES), F32)
    kf[0:_PAD, :] = zpad
    kf[_PAD + s:_PAD + s + _PAD, :] = zpad
    vf[0:_PAD, :] = zpad
    vf[_PAD + s:_PAD + s + _PAD, :] = zpad
    kf[_PAD:_PAD + s, :] = k_ref[...].astype(F32)
    vf[_PAD:_PAD + s, :] = v_ref[...].astype(F32)

    for br, d in enumerate(DILATIONS):
        n = s // d
        nt = n // _QT
        for c in range(d):
            if d == 1:
                qc[0:n, :] = q_ref[...]
                kc[0:n + 2 * BAND_HALF, :] = kf[_PAD - BAND_HALF:_PAD + s + BAND_HALF, :].astype(BF16)
                vc[0:n + 2 * BAND_HALF, :] = vf[_PAD - BAND_HALF:_PAD + s + BAND_HALF, :].astype(BF16)
            else:
                k0 = _PAD + c - BAND_HALF * d
                qc[0:n, :] = qf[pl.ds(c, n, stride=d), :].astype(BF16)
                kc[0:n + 2 * BAND_HALF, :] = kf[pl.ds(k0, n + 2 * BAND_HALF, stride=d), :].astype(BF16)
                vc[0:n + 2 * BAND_HALF, :] = vf[pl.ds(k0, n + 2 * BAND_HALF, stride=d), :].astype(BF16)

            def tile_body(t, carry, br=br, d=d, c=c, n=n):
                q0 = pl.multiple_of(t * _QT, _QT)
                qt = qc[pl.ds(q0, _QT), :]
                kt = kc[pl.ds(q0, _KT), :]
                vt = vc[pl.ds(q0, _KT), :]
                kpos = lax.broadcasted_iota(jnp.int32, (1, _KT), 1) + (q0 - BAND_HALF)
                valid = (kpos >= 0) & (kpos < n)
                outs, lses = [], []
                for h in range(2):
                    qh = jnp.where(head0 if h == 0 else jnp.logical_not(head0), qt, jnp.zeros_like(qt))
                    sc = _dot_nt(qh, kt) + bias_ref[br, h]
                    sc = jnp.where(valid, sc, NEG_INF)
                    m = jnp.max(sc, axis=-1, keepdims=True)
                    p = jnp.exp(sc - m)
                    l = jnp.sum(p, axis=-1, keepdims=True)
                    o = jnp.dot(p.astype(BF16), vt, preferred_element_type=F32)
                    outs.append(o * (1.0 / l))
                    lses.append(m + jnp.log(l))
                o = jnp.where(head0, outs[0], outs[1])
                lse = jnp.where(head0, lses[0], lses[1])
                if d == 1:
                    ob[br, pl.ds(q0, _QT), :] = o
                    lb[br, pl.ds(q0, _QT), :] = lse
                else:
                    ob[br, pl.ds(c + q0 * d, _QT, stride=d), :] = o
                    lb[br, pl.ds(c + q0 * d, _QT, stride=d), :] = lse
                return carry

            lax.fori_loop(0, nt, tile_body, 0)

    rows = 512
    for r in range(s // rows):
        sl = slice(r * rows, (r + 1) * rows)
        l0, l1, l2 = lb[0, sl, :], lb[1, sl, :], lb[2, sl, :]
        mx = jnp.maximum(jnp.maximum(l0, l1), l2)
        e0, e1, e2 = jnp.exp(l0 - mx), jnp.exp(l1 - mx), jnp.exp(l2 - mx)
        inv = 1.0 / (e0 + e1 + e2)
        out = (e0 * inv) * ob[0, sl, :] + (e1 * inv) * ob[1, sl, :] + (e2 * inv) * ob[2, sl, :]
        o_ref[sl, :] = out.astype(BF16)


def _dilated_attention(proj3, bias_tiles, batch, seq, batch_off):
    nblk = SEG // LANES
    return pl.pallas_call(
        functools.partial(_dilated_kernel, seq=seq),
        out_shape=jax.ShapeDtypeStruct((batch, seq, SEG), BF16),
        grid=(batch, nblk),
        in_specs=[
            pl.BlockSpec((None, seq, LANES), lambda b, h: (b + batch_off, 0, h)),
            pl.BlockSpec((None, seq, LANES), lambda b, h: (b + batch_off, 0, nblk + h)),
            pl.BlockSpec((None, seq, LANES), lambda b, h: (b + batch_off, 0, 2 * nblk + h)),
            pl.BlockSpec((3, 2, _QT, _KT), lambda b, h: (0, h, 0, 0)),
        ],
        out_specs=pl.BlockSpec((None, seq, LANES), lambda b, h: (b, 0, h)),
        scratch_shapes=[
            pltpu.VMEM((seq, LANES), F32),
            pltpu.VMEM((seq + 2 * _PAD, LANES), F32),
            pltpu.VMEM((seq + 2 * _PAD, LANES), F32),
            pltpu.VMEM((seq, LANES), BF16),
            pltpu.VMEM((seq + 2 * BAND_HALF, LANES), BF16),
            pltpu.VMEM((seq + 2 * BAND_HALF, LANES), BF16),
            pltpu.VMEM((3, seq, LANES), F32),
            pltpu.VMEM((3, seq, LANES), F32),
        ],
        compiler_params=pltpu.CompilerParams(
            dimension_semantics=("parallel", "parallel"), vmem_limit_bytes=VMEM_LIMIT),
        name="dilated_attn",
    )(proj3, proj3, proj3, bias_tiles)


_DQ = 256
_DB = 4


def _diff_kernel(lam_ref, q_ref, k_ref, v_ref, bias_ref, g_ref, o_ref,
                 s1_sc, s2_sc, m1_sc, m2_sc, l1_sc, l2_sc, a1_sc, a2_sc, *, nc):
    i = pl.program_id(2)
    lane = lax.broadcasted_iota(jnp.int32, (1, LANES), 1)
    first = lane < HEAD_DIM
    q = q_ref[...]
    q1 = jnp.where(first, q, jnp.zeros_like(q))
    q2 = jnp.where(first, jnp.zeros_like(q), q)

    m1_sc[...] = jnp.full((_DQ, LANES), -jnp.inf, F32)
    m2_sc[...] = jnp.full((_DQ, LANES), -jnp.inf, F32)
    zero = jnp.zeros((_DQ, LANES), F32)
    l1_sc[...] = zero
    l2_sc[...] = zero
    a1_sc[...] = zero
    a2_sc[...] = zero

    def score_body(j, carry):
        k0 = pl.multiple_of(j * _DQ, _DQ)
        kj = k_ref[pl.ds(k0, _DQ), :]
        b = bias_ref[jnp.clip(j - i, -_DB, _DB) + _DB]
        s1 = _dot_nt(q1, kj) + b
        s2 = _dot_nt(q2, kj) + b
        s1_sc[j] = s1
        s2_sc[j] = s2
        m1_sc[...] = jnp.maximum(m1_sc[...], jnp.maximum(s1[:, :LANES], s1[:, LANES:]))
        m2_sc[...] = jnp.maximum(m2_sc[...], jnp.maximum(s2[:, :LANES], s2[:, LANES:]))
        return carry

    lax.fori_loop(0, nc, score_body, 0)
    m1 = jnp.max(m1_sc[...], axis=-1, keepdims=True)
    m2 = jnp.max(m2_sc[...], axis=-1, keepdims=True)

    def pv_body(j, carry):
        k0 = pl.multiple_of(j * _DQ, _DQ)
        vj = v_ref[pl.ds(k0, _DQ), :]
        p1 = jnp.exp(s1_sc[j] - m1)
        p2 = jnp.exp(s2_sc[j] - m2)
        l1_sc[...] += p1[:, :LANES] + p1[:, LANES:]
        l2_sc[...] += p2[:, :LANES] + p2[:, LANES:]
        a1_sc[...] += jnp.dot(p1.astype(BF16), vj, preferred_element_type=F32)
        a2_sc[...] += jnp.dot(p2.astype(BF16), vj, preferred_element_type=F32)
        return carry

    lax.fori_loop(0, nc, pv_body, 0)
    l1 = jnp.sum(l1_sc[...], axis=-1, keepdims=True)
    l2 = jnp.sum(l2_sc[...], axis=-1, keepdims=True)
    lam = lam_ref[0]
    o = a1_sc[...] * (1.0 / l1) - (lam * a2_sc[...]) * (1.0 / l2)
    ms = jnp.mean(o * o, axis=-1, keepdims=True)
    o = ((o * lax.rsqrt(ms + NORM_EPS)) * g_ref[...]) * (1.0 - LAMBDA_INIT)
    o_ref[...] = o.astype(BF16)


def _diff_attention(lam, proj3, bias_tiles, gain, batch, seq, batch_off):
    nblk = SEG // LANES
    nc = seq // _DQ
    base = 3 * nblk
    return pl.pallas_call(
        functools.partial(_diff_kernel, nc=nc),
        out_shape=jax.ShapeDtypeStruct((batch, seq, SEG), BF16),
        grid=(batch, N_HEADS_B, nc),
        in_specs=[
            pl.BlockSpec(memory_space=pltpu.SMEM),
            pl.BlockSpec((None, _DQ, LANES), lambda b, h, i: (b + batch_off, i, base + h)),
            pl.BlockSpec((None, seq, LANES), lambda b, h, i: (b + batch_off, 0, base + nblk + h)),
            pl.BlockSpec((None, seq, LANES), lambda b, h, i: (b + batch_off, 0, base + 2 * nblk + h)),
            pl.BlockSpec((None, 2 * _DB + 1, _DQ, _DQ), lambda b, h, i: (h, 0, 0, 0)),
            pl.BlockSpec((1, LANES), lambda b, h, i: (0, 0)),
        ],
        out_specs=pl.BlockSpec((None, _DQ, LANES), lambda b, h, i: (b, i, h)),
        scratch_shapes=[
            pltpu.VMEM((nc, _DQ, _DQ), F32),
            pltpu.VMEM((nc, _DQ, _DQ), F32),
            pltpu.VMEM((_DQ, LANES), F32),
            pltpu.VMEM((_DQ, LANES), F32),
            pltpu.VMEM((_DQ, LANES), F32),
            pltpu.VMEM((_DQ, LANES), F32),
            pltpu.VMEM((_DQ, LANES), F32),
            pltpu.VMEM((_DQ, LANES), F32),
        ],
        compiler_params=pltpu.CompilerParams(
            dimension_semantics=("parallel", "parallel", "arbitrary"), vmem_limit_bytes=VMEM_LIMIT),
        name="diff_attn",
    )(lam, proj3, proj3, proj3, bias_tiles, gain)


def _out_proj_kernel(x_ref, oa_ref, ob_ref, wo_ref, g_ref, wq_ref, x2_ref, hn_ref, qp_ref):
    x2 = (x_ref[...]
          + jnp.dot(oa_ref[...], wo_ref[0:SEG, :], preferred_element_type=F32)
          + jnp.dot(ob_ref[...], wo_ref[SEG:2 * SEG, :], preferred_element_type=F32))
    x2_ref[...] = x2
    ms = jnp.mean(x2 * x2, axis=-1, keepdims=True)
    hn = (x2 * lax.rsqrt(ms + NORM_EPS)) * g_ref[...]
    hn_ref[...] = hn
    qp_ref[...] = jnp.dot(hn.astype(BF16), wq_ref[...], preferred_element_type=F32).astype(BF16)


def _out_proj(x, oa, ob, wo_bf, g, wq_bf, tm=512):
    t = x.shape[0]
    qw = wq_bf.shape[1]
    return pl.pallas_call(
        _out_proj_kernel,
        out_shape=(jax.ShapeDtypeStruct((t, D_MODEL), F32),
                   jax.ShapeDtypeStruct((t, D_MODEL), F32),
                   jax.ShapeDtypeStruct((t, qw), BF16)),
        grid=(t // tm,),
        in_specs=[
            pl.BlockSpec((tm, D_MODEL), lambda i: (i, 0)),
            pl.BlockSpec((tm, SEG), lambda i: (i, 0)),
            pl.BlockSpec((tm, SEG), lambda i: (i, 0)),
            pl.BlockSpec((2 * SEG, D_MODEL), lambda i: (0, 0)),
            pl.BlockSpec((1, D_MODEL), lambda i: (0, 0)),
            pl.BlockSpec((D_MODEL, qw), lambda i: (0, 0)),
        ],
        out_specs=(pl.BlockSpec((tm, D_MODEL), lambda i: (i, 0)),
                   pl.BlockSpec((tm, D_MODEL), lambda i: (i, 0)),
                   pl.BlockSpec((tm, qw), lambda i: (i, 0))),
        compiler_params=pltpu.CompilerParams(
            dimension_semantics=("parallel",), vmem_limit_bytes=VMEM_LIMIT),
        name="out_proj",
    )(x, oa, ob, wo_bf, g, wq_bf)


_CAND_ROWS = 16 + 7 * 8 + 8


def _cand_layout():
    flat = np.full((_CAND_ROWS,), -1, np.float32)
    flat[0:16] = np.arange(16)
    for a in range(1, 8):
        nb = PEER_TOPK // (a + 1)
        flat[16 + (a - 1) * 8:16 + (a - 1) * 8 + nb] = a * PEER_TOPK + np.arange(nb)
    flat[72:80] = np.arange(8, 16) * PEER_TOPK
    return flat


def _topk_kernel(qp_ref, sk_ref, cflat_ref, eidx_ref, gate_ref):
    tm = qp_ref.shape[0]
    kiota = lax.broadcasted_iota(jnp.int32, (N_KEYS, tm), 0).astype(F32)
    tv, ti = [], []
    for z in range(2):
        sc = _dot_nt(sk_ref[z], qp_ref[:, z * N_KEYS:(z + 1) * N_KEYS])
        vals, idxs = [], []
        for _ in range(PEER_TOPK):
            m = jnp.max(sc, axis=0, keepdims=True)
            idx = jnp.min(jnp.where(sc == m, kiota, float(N_KEYS)), axis=0, keepdims=True)
            vals.append(m)
            idxs.append(idx)
            sc = jnp.where(kiota == idx, -jnp.inf, sc)
        tv.append(vals)
        ti.append(idxs)

    v1_16 = jnp.concatenate(tv[1], axis=0)
    i1_16 = jnp.concatenate(ti[1], axis=0)
    v1_8, i1_8 = v1_16[0:8], i1_16[0:8]
    cand = [tv[0][0] + v1_16]
    cexp = [ti[0][0] * N_KEYS + i1_16]
    for a in range(1, 8):
        cand.append(tv[0][a] + v1_8)
        cexp.append(ti[0][a] * N_KEYS + i1_8)
    cand.append(jnp.concatenate(tv[0][8:16], axis=0) + tv[1][0])
    cexp.append(jnp.concatenate(ti[0][8:16], axis=0) * N_KEYS + ti[1][0])
    cand = jnp.concatenate(cand, axis=0)
    cexp = jnp.concatenate(cexp, axis=0)
    cflat = cflat_ref[...]
    real = cflat >= 0.0
    cand = jnp.where(real, cand, -jnp.inf)
    big = float(PEER_TOPK * PEER_TOPK)

    best, eids = [], []
    for _ in range(PEER_TOPK):
        m = jnp.max(cand, axis=0, keepdims=True)
        pos = jnp.min(jnp.where((cand == m) & real, cflat, big), axis=0, keepdims=True)
        sel = cflat == pos
        eids.append(jnp.max(jnp.where(sel, cexp, -1.0), axis=0, keepdims=True))
        best.append(m)
        cand = jnp.where(sel, -jnp.inf, cand)
    best = jnp.concatenate(best, axis=0)
    ex = jnp.exp(best - best[0:1])
    gate_ref[...] = ex * (1.0 / jnp.sum(ex, axis=0, keepdims=True))
    eidx_ref[...] = jnp.concatenate(eids, axis=0).astype(jnp.int32)


def _topk(qp, sk_bf, cflat_b, tm=512):
    t = qp.shape[0]
    return pl.pallas_call(
        _topk_kernel,
        out_shape=(jax.ShapeDtypeStruct((PICKS, t), jnp.int32),
                   jax.ShapeDtypeStruct((PICKS, t), F32)),
        grid=(t // tm, PEER_HEADS),
        in_specs=[
            pl.BlockSpec((tm, 2 * N_KEYS), lambda i, p: (i, p)),
            pl.BlockSpec((2, N_KEYS, N_KEYS), lambda i, p: (0, 0, 0)),
            pl.BlockSpec((_CAND_ROWS, tm), lambda i, p: (0, 0)),
        ],
        out_specs=(pl.BlockSpec((PEER_TOPK, tm), lambda i, p: (p, i)),
                   pl.BlockSpec((PEER_TOPK, tm), lambda i, p: (p, i))),
        compiler_params=pltpu.CompilerParams(
            dimension_semantics=("parallel", "parallel"), vmem_limit_bytes=VMEM_LIMIT),
        name="peer_topk",
    )(qp, sk_bf, cflat_b)


_HALF = D_MODEL // 2
_HROWS = _HALF // LANES
_TT = 64


def _unpack(words):
    lo = lax.bitcast_convert_type(words << 16, F32)
    hi = lax.bitcast_convert_type(words & jnp.uint32(0xFFFF0000), F32)
    return lo, hi


def _peer_act_kernel(idx_ref, hn_ref, tab_ref, gate_ref, w_ref, prod_sc):
    def token_body(t, carry):
        x = hn_ref[t]
        xlo, xhi = x[0:_HROWS], x[_HROWS:2 * _HROWS]
        base = t * PICKS
        for k in range(PICKS):
            lo, hi = _unpack(tab_ref[idx_ref[base + k]])
            prod_sc[k * _HROWS:(k + 1) * _HROWS, :] = lo * xlo + hi * xhi
        r = prod_sc[pl.ds(0, PICKS, stride=_HROWS), :]
        for s in range(1, _HROWS):
            r = r + prod_sc[pl.ds(s, PICKS, stride=_HROWS), :]
        w_ref[pl.ds(t, 1), :] = jnp.sum(r.T, axis=0, keepdims=True)
        return carry

    lax.fori_loop(0, _TT, token_body, 0)
    h = w_ref[...]
    act = 0.5 * h * (1.0 + lax.erf(h * (2.0 ** -0.5)))
    w_ref[...] = gate_ref[...] * act


def _peer_out_kernel(idx_ref, wsm_ref, x2_ref, tab_ref, y_ref):
    nacc = 4

    def token_body(t, carry):
        base = t * PICKS
        acc_lo = [jnp.zeros((_HROWS, LANES), F32) for _ in range(nacc)]
        acc_hi = [jnp.zeros((_HROWS, LANES), F32) for _ in range(nacc)]
        for k in range(PICKS):
            lo, hi = _unpack(tab_ref[idx_ref[base + k]])
            w = wsm_ref[base + k]
            acc_lo[k % nacc] = acc_lo[k % nacc] + w * lo
            acc_hi[k % nacc] = acc_hi[k % nacc] + w * hi
        lo = (acc_lo[0] + acc_lo[1]) + (acc_lo[2] + acc_lo[3])
        hi = (acc_hi[0] + acc_hi[1]) + (acc_hi[2] + acc_hi[3])
        x2 = x2_ref[t]
        y_ref[t, 0:_HROWS, :] = x2[0:_HROWS] + lo
        y_ref[t, _HROWS:2 * _HROWS, :] = x2[_HROWS:2 * _HROWS] + hi
        return carry

    lax.fori_loop(0, _TT, token_body, 0)


def _table_spec(n_exp):
    return pl.BlockSpec((n_exp, _HROWS, LANES), lambda i: (0, 0, 0), pipeline_mode=pl.Buffered(1))


def _peer_act(eidx_flat, hn3, tab, gate):
    t = hn3.shape[0]
    return pl.pallas_call(
        _peer_act_kernel,
        out_shape=jax.ShapeDtypeStruct((t, PICKS), F32),
        grid=(t // _TT,),
        in_specs=[
            pl.BlockSpec((_TT * PICKS,), lambda i: (i,), memory_space=pltpu.SMEM),
            pl.BlockSpec((_TT, 2 * _HROWS, LANES), lambda i: (i, 0, 0)),
            _table_spec(tab.shape[0]),
            pl.BlockSpec((_TT, PICKS), lambda i: (i, 0)),
        ],
        out_specs=pl.BlockSpec((_TT, PICKS), lambda i: (i, 0)),
        scratch_shapes=[pltpu.VMEM((PICKS * _HROWS, LANES), F32)],
        compiler_params=pltpu.CompilerParams(
            dimension_semantics=("parallel",), vmem_limit_bytes=VMEM_LIMIT),
        name="peer_act",
    )(eidx_flat, hn3, tab, gate)


def _peer_out(eidx_flat, w_flat, x23, tab):
    t = x23.shape[0]
    return pl.pallas_call(
        _peer_out_kernel,
        out_shape=jax.ShapeDtypeStruct(x23.shape, F32),
        grid=(t // _TT,),
        in_specs=[
            pl.BlockSpec((_TT * PICKS,), lambda i: (i,), memory_space=pltpu.SMEM),
            pl.BlockSpec((_TT * PICKS,), lambda i: (i,), memory_space=pltpu.SMEM),
            pl.BlockSpec((_TT, 2 * _HROWS, LANES), lambda i: (i, 0, 0)),
            _table_spec(tab.shape[0]),
        ],
        out_specs=pl.BlockSpec((_TT, 2 * _HROWS, LANES), lambda i: (i, 0, 0)),
        compiler_params=pltpu.CompilerParams(
            dimension_semantics=("parallel",), vmem_limit_bytes=VMEM_LIMIT),
        name="peer_out",
    )(eidx_flat, w_flat, x23, tab)


def _rel_bucket(rel):
    nb = NUM_BUCKETS // 2
    max_exact = nb // 2
    n = jnp.abs(rel)
    large = max_exact + (jnp.log(jnp.maximum(n, 1).astype(F32) / max_exact)
                         / math.log(MAX_DISTANCE / max_exact) * (nb - max_exact)).astype(jnp.int32)
    large = jnp.minimum(large, nb - 1)
    return (rel > 0).astype(jnp.int32) * nb + jnp.where(n < max_exact, n, large)


def _bucket_saturation():
    nb = NUM_BUCKETS // 2
    max_exact = nb // 2
    n = np.arange(1, 8192, dtype=np.float64)
    large = max_exact + np.floor(np.log(n / max_exact) / math.log(MAX_DISTANCE / max_exact) * (nb - max_exact))
    return int(n[np.argmax(large >= nb - 1)])


def _dilated_bias_tiles(bias_a):
    off = np.arange(_KT)[None, :] - BAND_HALF - np.arange(_QT)[:, None]
    in_band = jnp.asarray(np.abs(off) <= BAND_HALF)
    tiles = []
    for d in DILATIONS:
        b = bias_a[_rel_bucket(jnp.asarray(off * d, jnp.int32))]
        b = jnp.where(in_band[..., None], b.astype(F32), NEG_INF)
        tiles.append(b.transpose(2, 0, 1))
    return jnp.stack(tiles, 0)


def _diff_bias_tiles(bias_b):
    assert (_DB - 1) * _DQ + 1 >= _bucket_saturation() + _DQ // 2
    delta = np.arange(-_DB, _DB + 1)[:, None, None] * _DQ
    rel = delta + np.arange(_DQ)[None, None, :] - np.arange(_DQ)[None, :, None]
    b = bias_b[_rel_bucket(jnp.asarray(rel, jnp.int32))]
    return b.astype(F32).transpose(3, 0, 1, 2)


def _pack_table(w):
    e = w.shape[0]
    bits = lax.bitcast_convert_type(w.astype(BF16), jnp.uint16).astype(jnp.uint32)
    packed = bits[:, :_HALF] | (bits[:, _HALF:] << 16)
    return packed.reshape(e, _HROWS, LANES)


def _tile_gain(g, scale=1.0):
    return jnp.tile(g.astype(F32) * scale, SEG // HEAD_DIM).reshape(1, SEG)


def kernel(x_prompt, x_sample, rel_bias, attn_norm_g, w_in, q_norm_a, k_norm_a, q_norm_b, k_norm_b,
           lambda_q1, lambda_k1, lambda_q2, lambda_k2, diff_norm_g, w_out, ffn_norm_g,
           peer_w_q, peer_sub_keys, peer_u, peer_v):
    bp, sp, _ = x_prompt.shape
    bs, ss, _ = x_sample.shape
    tp, ts = bp * sp, bs * ss
    t = tp + ts
    assert tp % ss == 0 and tp % sp == 0

    x = jnp.concatenate([x_prompt.reshape(tp, D_MODEL), x_sample.reshape(ts, D_MODEL)], axis=0)

    gsum = jnp.asarray(np.kron(np.eye(SEG // HEAD_DIM), np.ones((HEAD_DIM, HEAD_DIM))), BF16)
    gains = jnp.stack([_tile_gain(q_norm_a[0], ATTN_SCALE), _tile_gain(k_norm_a[0]),
                       _tile_gain(q_norm_b[0], ATTN_SCALE), _tile_gain(k_norm_b[0])], 0)
    proj = _in_proj(x, attn_norm_g[0].reshape(1, D_MODEL).astype(F32), w_in[0].astype(BF16), gsum, gains)

    bias_a = _dilated_bias_tiles(rel_bias[:, :N_HEADS_A])
    bias_b = _diff_bias_tiles(rel_bias[:, N_HEADS_A:])
    lam = (jnp.exp(jnp.sum(lambda_q1[0].astype(F32) * lambda_k1[0].astype(F32)))
           - jnp.exp(jnp.sum(lambda_q2[0].astype(F32) * lambda_k2[0].astype(F32)))
           + LAMBDA_INIT).reshape(1)
    dgain = diff_norm_g[0].astype(F32).reshape(1, LANES)

    oa, ob = [], []
    for batch, seq, off in ((bp, sp, 0), (bs, ss, tp // ss)):
        proj3 = proj.reshape(t // seq, seq, IN_WIDTH)
        oa.append(_dilated_attention(proj3, bias_a, batch, seq, off).reshape(batch * seq, SEG))
        ob.append(_diff_attention(lam, proj3, bias_b, dgain, batch, seq, off).reshape(batch * seq, SEG))
    oa = jnp.concatenate(oa, axis=0)
    ob = jnp.concatenate(ob, axis=0)

    x2, hn, qp = _out_proj(x, oa, ob, w_out[0].astype(BF16),
                           ffn_norm_g[0].reshape(1, D_MODEL).astype(F32), peer_w_q[0].astype(BF16))

    tm = 512
    cflat_b = jnp.asarray(np.broadcast_to(_cand_layout()[:, None], (_CAND_ROWS, tm)).copy())
    eidx_t, gate_t = _topk(qp, peer_sub_keys[0].astype(BF16), cflat_b, tm)
    eidx = eidx_t.T
    gate = gate_t.T

    eidx_flat = eidx.reshape(t * PICKS)
    w = _peer_act(eidx_flat, hn.reshape(t, 2 * _HROWS, LANES), _pack_table(peer_u[0]), gate)
    y = _peer_out(eidx_flat, w.reshape(t * PICKS), x2.reshape(t, 2 * _HROWS, LANES), _pack_table(peer_v[0]))
    y = y.reshape(t, D_MODEL)
    return (y[:tp].reshape(bp, sp, D_MODEL), y[tp:].reshape(bs, ss, D_MODEL))
```

```python
import functools
import math

import numpy as np
import jax
import jax.numpy as jnp
from jax import lax
from jax.experimental import pallas as pl
from jax.experimental.pallas import tpu as pltpu

F32 = jnp.float32
BF16 = jnp.bfloat16

D_MODEL = 1024
HEAD_DIM = 64
N_HEADS_A = 8
N_HEADS_B = 4
SEG = 512
IN_WIDTH = 6 * SEG
LANES = 128
DILATIONS = (1, 4, 16)
BAND_HALF = 64
NUM_BUCKETS = 32
MAX_DISTANCE = 1024
N_KEYS = 128
PEER_HEADS = 8
PEER_TOPK = 16
PICKS = PEER_HEADS * PEER_TOPK
NORM_EPS = 1e-6
NEG_INF = -1e30
ATTN_SCALE = HEAD_DIM ** -0.5
LAMBDA_INIT = 0.8 - 0.6 * math.exp(-0.3 * 0)

VMEM_LIMIT = 48 * 1024 * 1024


def _dot_nt(a, b):
    return lax.dot_general(a, b, (((1,), (1,)), ((), ())), preferred_element_type=F32)


def _in_proj_kernel(x_ref, g_ref, w_ref, gsum_ref, gains_ref, o_ref):
    x = x_ref[...]
    ms = jnp.mean(x * x, axis=-1, keepdims=True)
    xn = ((x * lax.rsqrt(ms + NORM_EPS)) * g_ref[...]).astype(BF16)
    gi = 0
    for seg in range(6):
        acc = jnp.dot(xn, w_ref[:, seg * SEG:(seg + 1) * SEG], preferred_element_type=F32)
        if seg in (0, 1, 3, 4):
            sq = acc * acc
            hi = sq.astype(BF16)
            lo = (sq - hi.astype(F32)).astype(BF16)
            gs = (jnp.dot(hi, gsum_ref[...], preferred_element_type=F32)
                  + jnp.dot(lo, gsum_ref[...], preferred_element_type=F32))
            acc = (acc * lax.rsqrt(gs * (1.0 / HEAD_DIM) + NORM_EPS)) * gains_ref[gi]
            gi += 1
        o_ref[:, seg * SEG:(seg + 1) * SEG] = acc.astype(BF16)


def _in_proj(x, g, w_bf, gsum, gains, tm=512):
    t = x.shape[0]
    return pl.pallas_call(
        _in_proj_kernel,
        out_shape=jax.ShapeDtypeStruct((t, IN_WIDTH), BF16),
        grid=(t // tm,),
        in_specs=[
            pl.BlockSpec((tm, D_MODEL), lambda i: (i, 0)),
            pl.BlockSpec((1, D_MODEL), lambda i: (0, 0)),
            pl.BlockSpec((D_MODEL, IN_WIDTH), lambda i: (0, 0)),
            pl.BlockSpec((SEG, SEG), lambda i: (0, 0)),
            pl.BlockSpec((4, 1, SEG), lambda i: (0, 0, 0)),
        ],
        out_specs=pl.BlockSpec((tm, IN_WIDTH), lambda i: (i, 0)),
        compiler_params=pltpu.CompilerParams(
            dimension_semantics=("parallel",), vmem_limit_bytes=VMEM_LIMIT),
        name="in_proj",
    )(x, g, w_bf, gsum, gains)


_QT = 128
_KT = _QT + 2 * BAND_HALF
_PAD = BAND_HALF * max(DILATIONS)


def _dilated_kernel(q_ref, k_ref, v_ref, bias_ref, o_ref,
                    qf, kf, vf, qc, kc, vc, ob, lb, *, seq):
    s = seq
    lane = lax.broadcasted_iota(jnp.int32, (1, LANES), 1)
    head0 = lane < HEAD_DIM

    qf[...] = q_ref[...].astype(F32)
    zpad = jnp.zeros((_PAD, LANES), F32)
    kf[0:_PAD, :] = zpad
    kf[_PAD + s:_PAD + s + _PAD, :] = zpad
    vf[0:_PAD, :] = zpad
    vf[_PAD + s:_PAD + s + _PAD, :] = zpad
    kf[_PAD:_PAD + s, :] = k_ref[...].astype(F32)
    vf[_PAD:_PAD + s, :] = v_ref[...].astype(F32)

    for br, d in enumerate(DILATIONS):
        n = s // d
        nt = n // _QT
        for c in range(d):
            if d == 1:
                qc[0:n, :] = q_ref[...]
                kc[0:n + 2 * BAND_HALF, :] = kf[_PAD - BAND_HALF:_PAD + s + BAND_HALF, :].astype(BF16)
                vc[0:n + 2 * BAND_HALF, :] = vf[_PAD - BAND_HALF:_PAD + s + BAND_HALF, :].astype(BF16)
            else:
                k0 = _PAD + c - BAND_HALF * d
                qc[0:n, :] = qf[pl.ds(c, n, stride=d), :].astype(BF16)
                kc[0:n + 2 * BAND_HALF, :] = kf[pl.ds(k0, n + 2 * BAND_HALF, stride=d), :].astype(BF16)
                vc[0:n + 2 * BAND_HALF, :] = vf[pl.ds(k0, n + 2 * BAND_HALF, stride=d), :].astype(BF16)

            def tile_body(t, carry, br=br, d=d, c=c, n=n):
                q0 = pl.multiple_of(t * _QT, _QT)
                qt = qc[pl.ds(q0, _QT), :]
                kt = kc[pl.ds(q0, _KT), :]
                vt = vc[pl.ds(q0, _KT), :]
                kpos = lax.broadcasted_iota(jnp.int32, (1, _KT), 1) + (q0 - BAND_HALF)
                valid = (kpos >= 0) & (kpos < n)
                outs, lses = [], []
                for h in range(2):
                    qh = jnp.where(head0 if h == 0 else jnp.logical_not(head0), qt, jnp.zeros_like(qt))
                    sc = _dot_nt(qh, kt) + bias_ref[br, h]
                    sc = jnp.where(valid, sc, NEG_INF)
                    m = jnp.max(sc, axis=-1, keepdims=True)
                    p = jnp.exp(sc - m)
                    l = jnp.sum(p, axis=-1, keepdims=True)
                    o = jnp.dot(p.astype(BF16), vt, preferred_element_type=F32)
                    outs.append(o * (1.0 / l))
                    lses.append(m + jnp.log(l))
                o = jnp.where(head0, outs[0], outs[1])
                lse = jnp.where(head0, lses[0], lses[1])
                if d == 1:
                    ob[br, pl.ds(q0, _QT), :] = o
                    lb[br, pl.ds(q0, _QT), :] = lse
                else:
                    ob[br, pl.ds(c + q0 * d, _QT, stride=d), :] = o
                    lb[br, pl.ds(c + q0 * d, _QT, stride=d), :] = lse
                return carry

            lax.fori_loop(0, nt, tile_body, 0)

    rows = 512
    for r in range(s // rows):
        sl = slice(r * rows, (r + 1) * rows)
        l0, l1, l2 = lb[0, sl, :], lb[1, sl, :], lb[2, sl, :]
        mx = jnp.maximum(jnp.maximum(l0, l1), l2)
        e0, e1, e2 = jnp.exp(l0 - mx), jnp.exp(l1 - mx), jnp.exp(l2 - mx)
        inv = 1.0 / (e0 + e1 + e2)
        out = (e0 * inv) * ob[0, sl, :] + (e1 * inv) * ob[1, sl, :] + (e2 * inv) * ob[2, sl, :]
        o_ref[sl, :] = out.astype(BF16)


def _dilated_attention(proj3, bias_tiles, batch, seq, batch_off):
    nblk = SEG // LANES
    return pl.pallas_call(
        functools.partial(_dilated_kernel, seq=seq),
        out_shape=jax.ShapeDtypeStruct((batch, seq, SEG), BF16),
        grid=(batch, nblk),
        in_specs=[
            pl.BlockSpec((None, seq, LANES), lambda b, h: (b + batch_off, 0, h)),
            pl.BlockSpec((None, seq, LANES), lambda b, h: (b + batch_off, 0, nblk + h)),
            pl.BlockSpec((None, seq, LANES), lambda b, h: (b + batch_off, 0, 2 * nblk + h)),
            pl.BlockSpec((3, 2, _QT, _KT), lambda b, h: (0, h, 0, 0)),
        ],
        out_specs=pl.BlockSpec((None, seq, LANES), lambda b, h: (b, 0, h)),
        scratch_shapes=[
            pltpu.VMEM((seq, LANES), F32),
            pltpu.VMEM((seq + 2 * _PAD, LANES), F32),
            pltpu.VMEM((seq + 2 * _PAD, LANES), F32),
            pltpu.VMEM((seq, LANES), BF16),
            pltpu.VMEM((seq + 2 * BAND_HALF, LANES), BF16),
            pltpu.VMEM((seq + 2 * BAND_HALF, LANES), BF16),
            pltpu.VMEM((3, seq, LANES), F32),
            pltpu.VMEM((3, seq, LANES), F32),
        ],
        compiler_params=pltpu.CompilerParams(
            dimension_semantics=("parallel", "parallel"), vmem_limit_bytes=VMEM_LIMIT),
        name="dilated_attn",
    )(proj3, proj3, proj3, bias_tiles)


_DQ = 256
_DB = 4


def _diff_kernel(lam_ref, q_ref, k_ref, v_ref, bias_ref, g_ref, o_ref,
                 s1_sc, s2_sc, m1_sc, m2_sc, l1_sc, l2_sc, a1_sc, a2_sc, *, nc):
    i = pl.program_id(2)
    lane = lax.broadcasted_iota(jnp.int32, (1, LANES), 1)
    first = lane < HEAD_DIM
    q = q_ref[...]
    q1 = jnp.where(first, q, jnp.zeros_like(q))
    q2 = jnp.where(first, jnp.zeros_like(q), q)

    m1_sc[...] = jnp.full((_DQ, LANES), -jnp.inf, F32)
    m2_sc[...] = jnp.full((_DQ, LANES), -jnp.inf, F32)
    zero = jnp.zeros((_DQ, LANES), F32)
    l1_sc[...] = zero
    l2_sc[...] = zero
    a1_sc[...] = zero
    a2_sc[...] = zero

    def score_body(j, carry):
        k0 = pl.multiple_of(j * _DQ, _DQ)
        kj = k_ref[pl.ds(k0, _DQ), :]
        b = bias_ref[jnp.clip(j - i, -_DB, _DB) + _DB]
        s1 = _dot_nt(q1, kj) + b
        s2 = _dot_nt(q2, kj) + b
        s1_sc[j] = s1
        s2_sc[j] = s2
        m1_sc[...] = jnp.maximum(m1_sc[...], jnp.maximum(s1[:, :LANES], s1[:, LANES:]))
        m2_sc[...] = jnp.maximum(m2_sc[...], jnp.maximum(s2[:, :LANES], s2[:, LANES:]))
        return carry

    lax.fori_loop(0, nc, score_body, 0)
    m1 = jnp.max(m1_sc[...], axis=-1, keepdims=True)
    m2 = jnp.max(m2_sc[...], axis=-1, keepdims=True)

    def pv_body(j, carry):
        k0 = pl.multiple_of(j * _DQ, _DQ)
        vj = v_ref[pl.ds(k0, _DQ), :]
        p1 = jnp.exp(s1_sc[j] - m1)
        p2 = jnp.exp(s2_sc[j] - m2)
        l1_sc[...] += p1[:, :LANES] + p1[:, LANES:]
        l2_sc[...] += p2[:, :LANES] + p2[:, LANES:]
        a1_sc[...] += jnp.dot(p1.astype(BF16), vj, preferred_element_type=F32)
        a2_sc[...] += jnp.dot(p2.astype(BF16), vj, preferred_element_type=F32)
        return carry

    lax.fori_loop(0, nc, pv_body, 0)
    l1 = jnp.sum(l1_sc[...], axis=-1, keepdims=True)
    l2 = jnp.sum(l2_sc[...], axis=-1, keepdims=True)
    lam = lam_ref[0]
    o = a1_sc[...] * (1.0 / l1) - (lam * a2_sc[...]) * (1.0 / l2)
    ms = jnp.mean(o * o, axis=-1, keepdims=True)
    o = ((o * lax.rsqrt(ms + NORM_EPS)) * g_ref[...]) * (1.0 - LAMBDA_INIT)
    o_ref[...] = o.astype(BF16)


def _diff_attention(lam, proj3, bias_tiles, gain, batch, seq, batch_off):
    nblk = SEG // LANES
    nc = seq // _DQ
    base = 3 * nblk
    return pl.pallas_call(
        functools.partial(_diff_kernel, nc=nc),
        out_shape=jax.ShapeDtypeStruct((batch, seq, SEG), BF16),
        grid=(batch, N_HEADS_B, nc),
        in_specs=[
            pl.BlockSpec(memory_space=pltpu.SMEM),
            pl.BlockSpec((None, _DQ, LANES), lambda b, h, i: (b + batch_off, i, base + h)),
            pl.BlockSpec((None, seq, LANES), lambda b, h, i: (b + batch_off, 0, base + nblk + h)),
            pl.BlockSpec((None, seq, LANES), lambda b, h, i: (b + batch_off, 0, base + 2 * nblk + h)),
            pl.BlockSpec((None, 2 * _DB + 1, _DQ, _DQ), lambda b, h, i: (h, 0, 0, 0)),
            pl.BlockSpec((1, LANES), lambda b, h, i: (0, 0)),
        ],
        out_specs=pl.BlockSpec((None, _DQ, LANES), lambda b, h, i: (b, i, h)),
        scratch_shapes=[
            pltpu.VMEM((nc, _DQ, _DQ), F32),
            pltpu.VMEM((nc, _DQ, _DQ), F32),
            pltpu.VMEM((_DQ, LANES), F32),
            pltpu.VMEM((_DQ, LANES), F32),
            pltpu.VMEM((_DQ, LANES), F32),
            pltpu.VMEM((_DQ, LANES), F32),
            pltpu.VMEM((_DQ, LANES), F32),
            pltpu.VMEM((_DQ, LANES), F32),
        ],
        compiler_params=pltpu.CompilerParams(
            dimension_semantics=("parallel", "parallel", "arbitrary"), vmem_limit_bytes=VMEM_LIMIT),
        name="diff_attn",
    )(lam, proj3, proj3, proj3, bias_tiles, gain)


def _out_proj_kernel(x_ref, oa_ref, ob_ref, wo_ref, g_ref, wq_ref, x2_ref, hn_ref, qp_ref):
    x2 = (x_ref[...]
          + jnp.dot(oa_ref[...], wo_ref[0:SEG, :], preferred_element_type=F32)
          + jnp.dot(ob_ref[...], wo_ref[SEG:2 * SEG, :], preferred_element_type=F32))
    x2_ref[...] = x2
    ms = jnp.mean(x2 * x2, axis=-1, keepdims=True)
    hn = (x2 * lax.rsqrt(ms + NORM_EPS)) * g_ref[...]
    hn_ref[...] = hn
    qp_ref[...] = jnp.dot(hn.astype(BF16), wq_ref[...], preferred_element_type=F32).astype(BF16)


def _out_proj(x, oa, ob, wo_bf, g, wq_bf, tm=512):
    t = x.shape[0]
    qw = wq_bf.shape[1]
    return pl.pallas_call(
        _out_proj_kernel,
        out_shape=(jax.ShapeDtypeStruct((t, D_MODEL), F32),
                   jax.ShapeDtypeStruct((t, D_MODEL), F32),
                   jax.ShapeDtypeStruct((t, qw), BF16)),
        grid=(t // tm,),
        in_specs=[
            pl.BlockSpec((tm, D_MODEL), lambda i: (i, 0)),
            pl.BlockSpec((tm, SEG), lambda i: (i, 0)),
            pl.BlockSpec((tm, SEG), lambda i: (i, 0)),
            pl.BlockSpec((2 * SEG, D_MODEL), lambda i: (0, 0)),
            pl.BlockSpec((1, D_MODEL), lambda i: (0, 0)),
            pl.BlockSpec((D_MODEL, qw), lambda i: (0, 0)),
        ],
        out_specs=(pl.BlockSpec((tm, D_MODEL), lambda i: (i, 0)),
                   pl.BlockSpec((tm, D_MODEL), lambda i: (i, 0)),
                   pl.BlockSpec((tm, qw), lambda i: (i, 0))),
        compiler_params=pltpu.CompilerParams(
            dimension_semantics=("parallel",), vmem_limit_bytes=VMEM_LIMIT),
        name="out_proj",
    )(x, oa, ob, wo_bf, g, wq_bf)


_CAND_ROWS = 16 + 7 * 8 + 8


def _cand_layout():
    flat = np.full((_CAND_ROWS,), -1, np.float32)
    flat[0:16] = np.arange(16)
    for a in range(1, 8):
        nb = PEER_TOPK // (a + 1)
        flat[16 + (a - 1) * 8:16 + (a - 1) * 8 + nb] = a * PEER_TOPK + np.arange(nb)
    flat[72:80] = np.arange(8, 16) * PEER_TOPK
    return flat


def _topk_kernel(qp_ref, sk_ref, cflat_ref, eidx_ref, gate_ref):
    tm = qp_ref.shape[0]
    kiota = lax.broadcasted_iota(jnp.int32, (N_KEYS, tm), 0).astype(F32)
    tv, ti = [], []
    for z in range(2):
        sc = _dot_nt(sk_ref[z], qp_ref[:, z * N_KEYS:(z + 1) * N_KEYS])
        vals, idxs = [], []
        for _ in range(PEER_TOPK):
            m = jnp.max(sc, axis=0, keepdims=True)
            idx = jnp.min(jnp.where(sc == m, kiota, float(N_KEYS)), axis=0, keepdims=True)
            vals.append(m)
            idxs.append(idx)
            sc = jnp.where(kiota == idx, -jnp.inf, sc)
        tv.append(vals)
        ti.append(idxs)

    v1_16 = jnp.concatenate(tv[1], axis=0)
    i1_16 = jnp.concatenate(ti[1], axis=0)
    v1_8, i1_8 = v1_16[0:8], i1_16[0:8]
    cand = [tv[0][0] + v1_16]
    cexp = [ti[0][0] * N_KEYS + i1_16]
    for a in range(1, 8):
        cand.append(tv[0][a] + v1_8)
        cexp.append(ti[0][a] * N_KEYS + i1_8)
    cand.append(jnp.concatenate(tv[0][8:16], axis=0) + tv[1][0])
    cexp.append(jnp.concatenate(ti[0][8:16], axis=0) * N_KEYS + ti[1][0])
    cand = jnp.concatenate(cand, axis=0)
    cexp = jnp.concatenate(cexp, axis=0)
    cflat = cflat_ref[...]
    real = cflat >= 0.0
    cand = jnp.where(real, cand, -jnp.inf)
    big = float(PEER_TOPK * PEER_TOPK)

    best, eids = [], []
    for _ in range(PEER_TOPK):
        m = jnp.max(cand, axis=0, keepdims=True)
        pos = jnp.min(jnp.where((cand == m) & real, cflat, big), axis=0, keepdims=True)
        sel = cflat == pos
        eids.append(jnp.max(jnp.where(sel, cexp, -1.0), axis=0, keepdims=True))
        best.append(m)
        cand = jnp.where(sel, -jnp.inf, cand)
    best = jnp.concatenate(best, axis=0)
    ex = jnp.exp(best - best[0:1])
    gate_ref[...] = ex * (1.0 / jnp.sum(ex, axis=0, keepdims=True))
    eidx_ref[...] = jnp.concatenate(eids, axis=0).astype(jnp.int32)


def _topk(qp, sk_bf, cflat_b, tm=512):
    t = qp.shape[0]
    return pl.pallas_call(
        _topk_kernel,
        out_shape=(jax.ShapeDtypeStruct((PICKS, t), jnp.int32),
                   jax.ShapeDtypeStruct((PICKS, t), F32)),
        grid=(t // tm, PEER_HEADS),
        in_specs=[
            pl.BlockSpec((tm, 2 * N_KEYS), lambda i, p: (i, p)),
            pl.BlockSpec((2, N_KEYS, N_KEYS), lambda i, p: (0, 0, 0)),
            pl.BlockSpec((_CAND_ROWS, tm), lambda i, p: (0, 0)),
        ],
        out_specs=(pl.BlockSpec((PEER_TOPK, tm), lambda i, p: (p, i)),
                   pl.BlockSpec((PEER_TOPK, tm), lambda i, p: (p, i))),
        compiler_params=pltpu.CompilerParams(
            dimension_semantics=("parallel", "parallel"), vmem_limit_bytes=VMEM_LIMIT),
        name="peer_topk",
    )(qp, sk_bf, cflat_b)


_HALF = D_MODEL // 2
_HROWS = _HALF // LANES
_TT = 64
_GRP = 4
_NGRP = _TT // _GRP


def _gather_rows(idx_ref, tab_ref, stg, tok):
    row = idx_ref.at[pl.ds(pl.multiple_of(tok * PICKS, PICKS), PICKS)]
    for k in range(PICKS):
        stg[k * _HROWS:(k + 1) * _HROWS, :] = tab_ref[row[k]]


def _staged_matrix(stg):
    cols = [pltpu.bitcast(stg[pl.ds(s, PICKS, stride=_HROWS), :], BF16) for s in range(_HROWS)]
    return jnp.concatenate(cols, axis=1)


def _split_bf16(x):
    hi = x.astype(BF16)
    lo = (x - hi.astype(F32)).astype(BF16)
    return hi, lo


def _swap_pairs(x):
    even = (lax.broadcasted_iota(jnp.int32, (1, LANES), 1) & 1) == 0
    out = []
    for c in range(x.shape[1] // LANES):
        blk = x[:, c * LANES:(c + 1) * LANES]
        out.append(jnp.where(even, pltpu.roll(blk, LANES - 1, 1), pltpu.roll(blk, 1, 1)))
    return jnp.concatenate(out, axis=1)


def _peer_act_kernel(idx_ref, x_ref, tab_ref, gate_ref, w_ref, *stg):
    even = (lax.broadcasted_iota(jnp.int32, (1, 2 * PICKS), 1) & 1) == 0

    def gather_group(g):
        for j in range(_GRP):
            _gather_rows(idx_ref, tab_ref, stg[j], g * _GRP + j)

    def dot_group(g):
        r0 = pl.multiple_of(g * 2 * _GRP, 2 * _GRP)
        hi, lo = _split_bf16(x_ref[pl.ds(r0, 2 * _GRP), :])
        for j in range(_GRP):
            m = _staged_matrix(stg[j])
            r = _dot_nt(hi, m) + _dot_nt(lo, m)
            w_ref[pl.ds(g * _GRP + j, 1), :] = jnp.where(even, r[2 * j:2 * j + 1], r[2 * j + 1:2 * j + 2])

    gather_group(0)

    def body(g, carry):
        dot_group(g - 1)
        gather_group(g)
        return carry

    lax.fori_loop(1, _NGRP, body, 0)
    dot_group(_NGRP - 1)

    part = w_ref[...]
    h = part + _swap_pairs(part)
    act = 0.5 * h * (1.0 + lax.erf(h * (2.0 ** -0.5)))
    w_ref[...] = gate_ref[...] * act


def _peer_out_kernel(idx_ref, w_ref, x2_ref, tab_ref, y_ref, *stg):
    row = lax.broadcasted_iota(jnp.int32, (8, 2 * PICKS), 0)
    lane = lax.broadcasted_iota(jnp.int32, (8, 2 * PICKS), 1)
    keep = ((lane & 1) == (row & 1)) & (row < 4)

    def gather_group(g):
        for j in range(_GRP):
            _gather_rows(idx_ref, tab_ref, stg[j], g * _GRP + j)

    def dot_group(g):
        for j in range(_GRP):
            tok = g * _GRP + j
            w = jnp.broadcast_to(w_ref[pl.ds(tok, 1), :], (8, 2 * PICKS))
            hi = w.astype(BF16).astype(F32)
            a = jnp.where(keep, jnp.where(row < 2, hi, w - hi), 0.0).astype(BF16)
            r = jnp.dot(a, _staged_matrix(stg[j]), preferred_element_type=F32)
            y_ref[tok] = x2_ref[tok] + (r[0:2] + r[2:4])

    gather_group(0)

    def body(g, carry):
        dot_group(g - 1)
        gather_group(g)
        return carry

    lax.fori_loop(1, _NGRP, body, 0)
    dot_group(_NGRP - 1)


def _table_spec(n_exp):
    return pl.BlockSpec((n_exp, _HROWS, LANES), lambda i: (0, 0, 0), pipeline_mode=pl.Buffered(1))


def _staging():
    return [pltpu.VMEM((PICKS * _HROWS, LANES), jnp.uint32) for _ in range(_GRP)]


def _peer_act(eidx_flat, x2d, tab, gate2):
    t = gate2.shape[0]
    return pl.pallas_call(
        _peer_act_kernel,
        out_shape=jax.ShapeDtypeStruct((t, 2 * PICKS), F32),
        grid=(t // _TT,),
        in_specs=[
            pl.BlockSpec((_TT * PICKS,), lambda i: (i,), memory_space=pltpu.SMEM),
            pl.BlockSpec((2 * _TT, _HALF), lambda i: (i, 0)),
            _table_spec(tab.shape[0]),
            pl.BlockSpec((_TT, 2 * PICKS), lambda i: (i, 0)),
        ],
        out_specs=pl.BlockSpec((_TT, 2 * PICKS), lambda i: (i, 0)),
        scratch_shapes=_staging(),
        compiler_params=pltpu.CompilerParams(
            dimension_semantics=("parallel",), vmem_limit_bytes=VMEM_LIMIT),
        name="peer_act",
    )(eidx_flat, x2d, tab, gate2)


def _peer_out(eidx_flat, w2, x23, tab):
    t = x23.shape[0]
    return pl.pallas_call(
        _peer_out_kernel,
        out_shape=jax.ShapeDtypeStruct(x23.shape, F32),
        grid=(t // _TT,),
        in_specs=[
            pl.BlockSpec((_TT * PICKS,), lambda i: (i,), memory_space=pltpu.SMEM),
            pl.BlockSpec((_TT, 2 * PICKS), lambda i: (i, 0)),
            pl.BlockSpec((_TT, 2, _HALF), lambda i: (i, 0, 0)),
            _table_spec(tab.shape[0]),
        ],
        out_specs=pl.BlockSpec((_TT, 2, _HALF), lambda i: (i, 0, 0)),
        scratch_shapes=_staging(),
        compiler_params=pltpu.CompilerParams(
            dimension_semantics=("parallel",), vmem_limit_bytes=VMEM_LIMIT),
        name="peer_out",
    )(eidx_flat, w2, x23, tab)


def _rel_bucket(rel):
    nb = NUM_BUCKETS // 2
    max_exact = nb // 2
    n = jnp.abs(rel)
    large = max_exact + (jnp.log(jnp.maximum(n, 1).astype(F32) / max_exact)
                         / math.log(MAX_DISTANCE / max_exact) * (nb - max_exact)).astype(jnp.int32)
    large = jnp.minimum(large, nb - 1)
    return (rel > 0).astype(jnp.int32) * nb + jnp.where(n < max_exact, n, large)


def _bucket_saturation():
    nb = NUM_BUCKETS // 2
    max_exact = nb // 2
    n = np.arange(1, 8192, dtype=np.float64)
    large = max_exact + np.floor(np.log(n / max_exact) / math.log(MAX_DISTANCE / max_exact) * (nb - max_exact))
    return int(n[np.argmax(large >= nb - 1)])


def _toeplitz(v, nq, nk, k0):
    h = v.shape[0]
    width = nq + nk - 1
    w = v[:, k0 - (nq - 1):k0 + nk]
    w = jnp.concatenate([w, jnp.zeros((h, 1), v.dtype)], axis=1)
    rows = jnp.tile(w, (1, nq))[:, :nq * width].reshape(h, nq, width)
    return rows[:, :, nq - 1:nq - 1 + nk]


def _dilated_bias_tiles(bias_a):
    reach = _QT + BAND_HALF
    off = np.arange(-reach, reach + 1)
    tiles = []
    for d in DILATIONS:
        v = bias_a[_rel_bucket(jnp.asarray(off * d, jnp.int32))].astype(F32)
        v = jnp.where(jnp.asarray(np.abs(off) <= BAND_HALF)[:, None], v, NEG_INF)
        tiles.append(_toeplitz(v.T, _QT, _KT, reach - BAND_HALF))
    return jnp.stack(tiles, 0)


def _diff_bias_tiles(bias_b):
    assert (_DB - 1) * _DQ + 1 >= _bucket_saturation() + _DQ // 2
    reach = (_DB + 1) * _DQ
    rel = np.arange(-reach, reach + 1)
    v = bias_b[_rel_bucket(jnp.asarray(rel, jnp.int32))].astype(F32).T
    tiles = [_toeplitz(v, _DQ, _DQ, reach + delta * _DQ) for delta in range(-_DB, _DB + 1)]
    return jnp.stack(tiles, 1)


def _pack_table(w):
    e = w.shape[0]
    bits = lax.bitcast_convert_type(w.astype(BF16), jnp.uint16).astype(jnp.uint32)
    packed = bits[:, :_HALF] | (bits[:, _HALF:] << 16)
    return packed.reshape(e, _HROWS, LANES)


def _tile_gain(g, scale=1.0):
    return jnp.tile(g.astype(F32) * scale, SEG // HEAD_DIM).reshape(1, SEG)


def kernel(x_prompt, x_sample, rel_bias, attn_norm_g, w_in, q_norm_a, k_norm_a, q_norm_b, k_norm_b,
           lambda_q1, lambda_k1, lambda_q2, lambda_k2, diff_norm_g, w_out, ffn_norm_g,
           peer_w_q, peer_sub_keys, peer_u, peer_v):
    bp, sp, _ = x_prompt.shape
    bs, ss, _ = x_sample.shape
    tp, ts = bp * sp, bs * ss
    t = tp + ts
    assert tp % ss == 0 and tp % sp == 0

    x = jnp.concatenate([x_prompt.reshape(tp, D_MODEL), x_sample.reshape(ts, D_MODEL)], axis=0)

    gsum = jnp.asarray(np.kron(np.eye(SEG // HEAD_DIM), np.ones((HEAD_DIM, HEAD_DIM))), BF16)
    gains = jnp.stack([_tile_gain(q_norm_a[0], ATTN_SCALE), _tile_gain(k_norm_a[0]),
                       _tile_gain(q_norm_b[0], ATTN_SCALE), _tile_gain(k_norm_b[0])], 0)
    proj = _in_proj(x, attn_norm_g[0].reshape(1, D_MODEL).astype(F32), w_in[0].astype(BF16), gsum, gains)

    bias_a = _dilated_bias_tiles(rel_bias[:, :N_HEADS_A])
    bias_b = _diff_bias_tiles(rel_bias[:, N_HEADS_A:])
    lam = (jnp.exp(jnp.sum(lambda_q1[0].astype(F32) * lambda_k1[0].astype(F32)))
           - jnp.exp(jnp.sum(lambda_q2[0].astype(F32) * lambda_k2[0].astype(F32)))
           + LAMBDA_INIT).reshape(1)
    dgain = diff_norm_g[0].astype(F32).reshape(1, LANES)

    oa, ob = [], []
    for batch, seq, off in ((bp, sp, 0), (bs, ss, tp // ss)):
        proj3 = proj.reshape(t // seq, seq, IN_WIDTH)
        oa.append(_dilated_attention(proj3, bias_a, batch, seq, off).reshape(batch * seq, SEG))
        ob.append(_diff_attention(lam, proj3, bias_b, dgain, batch, seq, off).reshape(batch * seq, SEG))
    oa = jnp.concatenate(oa, axis=0)
    ob = jnp.concatenate(ob, axis=0)

    x2, hn, qp = _out_proj(x, oa, ob, w_out[0].astype(BF16),
                           ffn_norm_g[0].reshape(1, D_MODEL).astype(F32), peer_w_q[0].astype(BF16))

    tm = 512
    cflat_b = jnp.asarray(np.broadcast_to(_cand_layout()[:, None], (_CAND_ROWS, tm)).copy())
    eidx_t, gate_t = _topk(qp, peer_sub_keys[0].astype(BF16), cflat_b, tm)
    eidx_flat = eidx_t.T.reshape(t * PICKS)
    gate2 = jnp.repeat(gate_t.T, 2, axis=1)

    w2 = _peer_act(eidx_flat, hn.reshape(2 * t, _HALF), _pack_table(peer_u[0]), gate2)
    y = _peer_out(eidx_flat, w2, x2.reshape(t, 2, _HALF), _pack_table(peer_v[0]))
    y = y.reshape(t, D_MODEL)
    return (y[:tp].reshape(bp, sp, D_MODEL), y[tp:].reshape(bs, ss, D_MODEL))
```

```python
import functools
import math

import numpy as np
import jax
import jax.numpy as jnp
from jax import lax
from jax.experimental import pallas as pl
from jax.experimental.pallas import tpu as pltpu

F32 = jnp.float32
BF16 = jnp.bfloat16

D_MODEL = 1024
HEAD_DIM = 64
N_HEADS_A = 8
N_HEADS_B = 4
SEG = 512
IN_WIDTH = 6 * SEG
LANES = 128
DILATIONS = (1, 4, 16)
BAND_HALF = 64
NUM_BUCKETS = 32
MAX_DISTANCE = 1024
N_KEYS = 128
PEER_HEADS = 8
PEER_TOPK = 16
PICKS = PEER_HEADS * PEER_TOPK
NORM_EPS = 1e-6
NEG_INF = -1e30
ATTN_SCALE = HEAD_DIM ** -0.5
LAMBDA_INIT = 0.8 - 0.6 * math.exp(-0.3 * 0)

VMEM_LIMIT = 48 * 1024 * 1024


def _dot_nt(a, b):
    return lax.dot_general(a, b, (((1,), (1,)), ((), ())), preferred_element_type=F32)


def _in_proj_kernel(x_ref, g_ref, w_ref, gsum_ref, gains_ref, o_ref):
    x = x_ref[...]
    ms = jnp.mean(x * x, axis=-1, keepdims=True)
    xn = ((x * lax.rsqrt(ms + NORM_EPS)) * g_ref[...]).astype(BF16)
    gi = 0
    for seg in range(6):
        acc = jnp.dot(xn, w_ref[:, seg * SEG:(seg + 1) * SEG], preferred_element_type=F32)
        if seg in (0, 1, 3, 4):
            sq = acc * acc
            hi = sq.astype(BF16)
            lo = (sq - hi.astype(F32)).astype(BF16)
            gs = (jnp.dot(hi, gsum_ref[...], preferred_element_type=F32)
                  + jnp.dot(lo, gsum_ref[...], preferred_element_type=F32))
            acc = (acc * lax.rsqrt(gs * (1.0 / HEAD_DIM) + NORM_EPS)) * gains_ref[gi]
            gi += 1
        o_ref[:, seg * SEG:(seg + 1) * SEG] = acc.astype(BF16)


def _in_proj(x, g, w_bf, gsum, gains, tm=512):
    t = x.shape[0]
    return pl.pallas_call(
        _in_proj_kernel,
        out_shape=jax.ShapeDtypeStruct((t, IN_WIDTH), BF16),
        grid=(t // tm,),
        in_specs=[
            pl.BlockSpec((tm, D_MODEL), lambda i: (i, 0)),
            pl.BlockSpec((1, D_MODEL), lambda i: (0, 0)),
            pl.BlockSpec((D_MODEL, IN_WIDTH), lambda i: (0, 0)),
            pl.BlockSpec((SEG, SEG), lambda i: (0, 0)),
            pl.BlockSpec((4, 1, SEG), lambda i: (0, 0, 0)),
        ],
        out_specs=pl.BlockSpec((tm, IN_WIDTH), lambda i: (i, 0)),
        compiler_params=pltpu.CompilerParams(
            dimension_semantics=("parallel",), vmem_limit_bytes=VMEM_LIMIT),
        name="in_proj",
    )(x, g, w_bf, gsum, gains)


_QT = 128
_KT = _QT + 2 * BAND_HALF
_PAD = BAND_HALF * max(DILATIONS)


def _dilated_kernel(q_ref, k_ref, v_ref, bias_ref, o_ref,
                    qf, kf, vf, qc, kc, vc, ob, lb, *, seq):
    s = seq
    lane = lax.broadcasted_iota(jnp.int32, (1, LANES), 1)
    head0 = lane < HEAD_DIM

    qf[...] = q_ref[...].astype(F32)
    zpad = jnp.zeros((_PAD, LANES), F32)
    kf[0:_PAD, :] = zpad
    kf[_PAD + s:_PAD + s + _PAD, :] = zpad
    vf[0:_PAD, :] = zpad
    vf[_PAD + s:_PAD + s + _PAD, :] = zpad
    kf[_PAD:_PAD + s, :] = k_ref[...].astype(F32)
    vf[_PAD:_PAD + s, :] = v_ref[...].astype(F32)

    for br, d in enumerate(DILATIONS):
        n = s // d
        nt = n // _QT
        for c in range(d):
            if d == 1:
                qc[0:n, :] = q_ref[...]
                kc[0:n + 2 * BAND_HALF, :] = kf[_PAD - BAND_HALF:_PAD + s + BAND_HALF, :].astype(BF16)
                vc[0:n + 2 * BAND_HALF, :] = vf[_PAD - BAND_HALF:_PAD + s + BAND_HALF, :].astype(BF16)
            else:
                k0 = _PAD + c - BAND_HALF * d
                qc[0:n, :] = qf[pl.ds(c, n, stride=d), :].astype(BF16)
                kc[0:n + 2 * BAND_HALF, :] = kf[pl.ds(k0, n + 2 * BAND_HALF, stride=d), :].astype(BF16)
                vc[0:n + 2 * BAND_HALF, :] = vf[pl.ds(k0, n + 2 * BAND_HALF, stride=d), :].astype(BF16)

            def tile(t, br=br, d=d, c=c, n=n):
                q0 = t * _QT if isinstance(t, int) else pl.multiple_of(t * _QT, _QT)
                qt = qc[pl.ds(q0, _QT), :]
                kt = kc[pl.ds(q0, _KT), :]
                vt = vc[pl.ds(q0, _KT), :]
                kpos = lax.broadcasted_iota(jnp.int32, (1, _KT), 1) + (q0 - BAND_HALF)
                valid = (kpos >= 0) & (kpos < n)
                outs, lses = [], []
                for h in range(2):
                    qh = jnp.where(head0 if h == 0 else jnp.logical_not(head0), qt, jnp.zeros_like(qt))
                    sc = _dot_nt(qh, kt) + bias_ref[br, h]
                    sc = jnp.where(valid, sc, NEG_INF)
                    m = jnp.max(sc, axis=-1, keepdims=True)
                    p = jnp.exp(sc - m)
                    l = jnp.sum(p, axis=-1, keepdims=True)
                    o = jnp.dot(p.astype(BF16), vt, preferred_element_type=F32)
                    outs.append(o * (1.0 / l))
                    lses.append(m + jnp.log(l))
                o = jnp.where(head0, outs[0], outs[1])
                lse = jnp.where(head0, lses[0], lses[1])
                if d == 1:
                    ob[br, pl.ds(q0, _QT), :] = o
                    lb[br, pl.ds(q0, _QT), :] = lse
                else:
                    ob[br, pl.ds(c + q0 * d, _QT, stride=d), :] = o
                    lb[br, pl.ds(c + q0 * d, _QT, stride=d), :] = lse

            def tile_pair(t2, carry, tile=tile):
                tile(2 * t2)
                tile(2 * t2 + 1)
                return carry

            if nt % 2 == 0:
                lax.fori_loop(0, nt // 2, tile_pair, 0)
            else:
                assert nt == 1
                tile(0)

    rows = 512
    for r in range(s // rows):
        sl = slice(r * rows, (r + 1) * rows)
        l0, l1, l2 = lb[0, sl, :], lb[1, sl, :], lb[2, sl, :]
        mx = jnp.maximum(jnp.maximum(l0, l1), l2)
        e0, e1, e2 = jnp.exp(l0 - mx), jnp.exp(l1 - mx), jnp.exp(l2 - mx)
        inv = 1.0 / (e0 + e1 + e2)
        out = (e0 * inv) * ob[0, sl, :] + (e1 * inv) * ob[1, sl, :] + (e2 * inv) * ob[2, sl, :]
        o_ref[sl, :] = out.astype(BF16)


def _dilated_attention(proj3, bias_tiles, batch, seq):
    nblk = SEG // LANES
    return pl.pallas_call(
        functools.partial(_dilated_kernel, seq=seq),
        out_shape=jax.ShapeDtypeStruct((batch, seq, SEG), BF16),
        grid=(batch, nblk),
        in_specs=[
            pl.BlockSpec((None, seq, LANES), lambda b, h: (b,0, h)),
            pl.BlockSpec((None, seq, LANES), lambda b, h: (b,0, nblk + h)),
            pl.BlockSpec((None, seq, LANES), lambda b, h: (b,0, 2 * nblk + h)),
            pl.BlockSpec((3, 2, _QT, _KT), lambda b, h: (0, h, 0, 0)),
        ],
        out_specs=pl.BlockSpec((None, seq, LANES), lambda b, h: (b, 0, h)),
        scratch_shapes=[
            pltpu.VMEM((seq, LANES), F32),
            pltpu.VMEM((seq + 2 * _PAD, LANES), F32),
            pltpu.VMEM((seq + 2 * _PAD, LANES), F32),
            pltpu.VMEM((seq, LANES), BF16),
            pltpu.VMEM((seq + 2 * BAND_HALF, LANES), BF16),
            pltpu.VMEM((seq + 2 * BAND_HALF, LANES), BF16),
            pltpu.VMEM((3, seq, LANES), F32),
            pltpu.VMEM((3, seq, LANES), F32),
        ],
        compiler_params=pltpu.CompilerParams(
            dimension_semantics=("parallel", "parallel"), vmem_limit_bytes=VMEM_LIMIT),
        name="dilated_attn",
    )(proj3, proj3, proj3, bias_tiles)


_DQ = 256
_DB = 4


def _diff_kernel(lam_ref, q_ref, k_ref, v_ref, bias_ref, g_ref, o_ref,
                 s1_sc, s2_sc, m1_sc, m2_sc, l1_sc, l2_sc, a1_sc, a2_sc, *, nc):
    i = pl.program_id(2)
    lane = lax.broadcasted_iota(jnp.int32, (1, LANES), 1)
    first = lane < HEAD_DIM
    q = q_ref[...]
    q1 = jnp.where(first, q, jnp.zeros_like(q))
    q2 = jnp.where(first, jnp.zeros_like(q), q)

    m1_sc[...] = jnp.full((_DQ, LANES), -jnp.inf, F32)
    m2_sc[...] = jnp.full((_DQ, LANES), -jnp.inf, F32)
    zero = jnp.zeros((_DQ, LANES), F32)
    l1_sc[...] = zero
    l2_sc[...] = zero
    a1_sc[...] = zero
    a2_sc[...] = zero

    def lane_fold(x, op):
        parts = [x[:, c * LANES:(c + 1) * LANES] for c in range(x.shape[1] // LANES)]
        while len(parts) > 1:
            parts = [op(parts[a], parts[a + 1]) for a in range(0, len(parts), 2)]
        return parts[0]

    def score_body(jj, carry):
        j = 2 * jj
        k0 = pl.multiple_of(j * _DQ, 2 * _DQ)
        kj = k_ref[pl.ds(k0, 2 * _DQ), :]
        b = jnp.concatenate([bias_ref[jnp.clip(j - i, -_DB, _DB) + _DB],
                             bias_ref[jnp.clip(j + 1 - i, -_DB, _DB) + _DB]], axis=1)
        s1 = _dot_nt(q1, kj) + b
        s2 = _dot_nt(q2, kj) + b
        s1_sc[jj] = s1
        s2_sc[jj] = s2
        m1_sc[...] = jnp.maximum(m1_sc[...], lane_fold(s1, jnp.maximum))
        m2_sc[...] = jnp.maximum(m2_sc[...], lane_fold(s2, jnp.maximum))
        return carry

    lax.fori_loop(0, nc // 2, score_body, 0)
    m1 = jnp.max(m1_sc[...], axis=-1, keepdims=True)
    m2 = jnp.max(m2_sc[...], axis=-1, keepdims=True)

    def pv_body(jj, carry):
        k0 = pl.multiple_of(jj * 2 * _DQ, 2 * _DQ)
        vj = v_ref[pl.ds(k0, 2 * _DQ), :]
        p1 = jnp.exp(s1_sc[jj] - m1)
        p2 = jnp.exp(s2_sc[jj] - m2)
        l1_sc[...] += lane_fold(p1, jnp.add)
        l2_sc[...] += lane_fold(p2, jnp.add)
        a1_sc[...] += jnp.dot(p1.astype(BF16), vj, preferred_element_type=F32)
        a2_sc[...] += jnp.dot(p2.astype(BF16), vj, preferred_element_type=F32)
        return carry

    lax.fori_loop(0, nc // 2, pv_body, 0)
    l1 = jnp.sum(l1_sc[...], axis=-1, keepdims=True)
    l2 = jnp.sum(l2_sc[...], axis=-1, keepdims=True)
    lam = lam_ref[0]
    o = a1_sc[...] * (1.0 / l1) - (lam * a2_sc[...]) * (1.0 / l2)
    ms = jnp.mean(o * o, axis=-1, keepdims=True)
    o = ((o * lax.rsqrt(ms + NORM_EPS)) * g_ref[...]) * (1.0 - LAMBDA_INIT)
    o_ref[...] = o.astype(BF16)


def _diff_attention(lam, proj3, bias_tiles, gain, batch, seq):
    nblk = SEG // LANES
    nc = seq // _DQ
    base = 3 * nblk
    return pl.pallas_call(
        functools.partial(_diff_kernel, nc=nc),
        out_shape=jax.ShapeDtypeStruct((batch, seq, SEG), BF16),
        grid=(batch, N_HEADS_B, nc),
        in_specs=[
            pl.BlockSpec(memory_space=pltpu.SMEM),
            pl.BlockSpec((None, _DQ, LANES), lambda b, h, i: (b,i, base + h)),
            pl.BlockSpec((None, seq, LANES), lambda b, h, i: (b,0, base + nblk + h)),
            pl.BlockSpec((None, seq, LANES), lambda b, h, i: (b,0, base + 2 * nblk + h)),
            pl.BlockSpec((None, 2 * _DB + 1, _DQ, _DQ), lambda b, h, i: (h, 0, 0, 0)),
            pl.BlockSpec((1, LANES), lambda b, h, i: (0, 0)),
        ],
        out_specs=pl.BlockSpec((None, _DQ, LANES), lambda b, h, i: (b, i, h)),
        scratch_shapes=[
            pltpu.VMEM((nc // 2, _DQ, 2 * _DQ), F32),
            pltpu.VMEM((nc // 2, _DQ, 2 * _DQ), F32),
            pltpu.VMEM((_DQ, LANES), F32),
            pltpu.VMEM((_DQ, LANES), F32),
            pltpu.VMEM((_DQ, LANES), F32),
            pltpu.VMEM((_DQ, LANES), F32),
            pltpu.VMEM((_DQ, LANES), F32),
            pltpu.VMEM((_DQ, LANES), F32),
        ],
        compiler_params=pltpu.CompilerParams(
            dimension_semantics=("parallel", "parallel", "arbitrary"), vmem_limit_bytes=VMEM_LIMIT),
        name="diff_attn",
    )(lam, proj3, proj3, proj3, bias_tiles, gain)


def _out_proj_kernel(x_ref, oa_ref, ob_ref, wo_ref, g_ref, wq_ref, x2_ref, hn_ref, qp_ref):
    x2 = (x_ref[...]
          + jnp.dot(oa_ref[...], wo_ref[0:SEG, :], preferred_element_type=F32)
          + jnp.dot(ob_ref[...], wo_ref[SEG:2 * SEG, :], preferred_element_type=F32))
    x2_ref[...] = x2
    ms = jnp.mean(x2 * x2, axis=-1, keepdims=True)
    hn = (x2 * lax.rsqrt(ms + NORM_EPS)) * g_ref[...]
    hn_ref[...] = hn
    qp_ref[...] = jnp.dot(hn.astype(BF16), wq_ref[...], preferred_element_type=F32).astype(BF16)


def _out_proj(x, oa, ob, wo_bf, g, wq_bf, tm=512):
    t = x.shape[0]
    qw = wq_bf.shape[1]
    return pl.pallas_call(
        _out_proj_kernel,
        out_shape=(jax.ShapeDtypeStruct((t, D_MODEL), F32),
                   jax.ShapeDtypeStruct((t, D_MODEL), F32),
                   jax.ShapeDtypeStruct((t, qw), BF16)),
        grid=(t // tm,),
        in_specs=[
            pl.BlockSpec((tm, D_MODEL), lambda i: (i, 0)),
            pl.BlockSpec((tm, SEG), lambda i: (i, 0)),
            pl.BlockSpec((tm, SEG), lambda i: (i, 0)),
            pl.BlockSpec((2 * SEG, D_MODEL), lambda i: (0, 0)),
            pl.BlockSpec((1, D_MODEL), lambda i: (0, 0)),
            pl.BlockSpec((D_MODEL, qw), lambda i: (0, 0)),
        ],
        out_specs=(pl.BlockSpec((tm, D_MODEL), lambda i: (i, 0)),
                   pl.BlockSpec((tm, D_MODEL), lambda i: (i, 0)),
                   pl.BlockSpec((tm, qw), lambda i: (i, 0))),
        compiler_params=pltpu.CompilerParams(
            dimension_semantics=("parallel",), vmem_limit_bytes=VMEM_LIMIT),
        name="out_proj",
    )(x, oa, ob, wo_bf, g, wq_bf)


_CAND_ROWS = 16 + 7 * 8 + 8


def _cand_layout():
    flat = np.full((_CAND_ROWS,), -1, np.float32)
    flat[0:16] = np.arange(16)
    for a in range(1, 8):
        nb = PEER_TOPK // (a + 1)
        flat[16 + (a - 1) * 8:16 + (a - 1) * 8 + nb] = a * PEER_TOPK + np.arange(nb)
    flat[72:80] = np.arange(8, 16) * PEER_TOPK
    return flat


def _topk_kernel(qp_ref, sk_ref, cflat_ref, eidx_ref, gate_ref):
    tm = qp_ref.shape[0]
    kiota = lax.broadcasted_iota(jnp.int32, (N_KEYS, tm), 0).astype(F32)
    tv, ti = [], []
    for z in range(2):
        sc = _dot_nt(sk_ref[z], qp_ref[:, z * N_KEYS:(z + 1) * N_KEYS])
        vals, idxs = [], []
        for _ in range(PEER_TOPK):
            m = jnp.max(sc, axis=0, keepdims=True)
            idx = jnp.min(jnp.where(sc == m, kiota, float(N_KEYS)), axis=0, keepdims=True)
            vals.append(m)
            idxs.append(idx)
            sc = jnp.where(kiota == idx, -jnp.inf, sc)
        tv.append(vals)
        ti.append(idxs)

    v1_16 = jnp.concatenate(tv[1], axis=0)
    i1_16 = jnp.concatenate(ti[1], axis=0)
    v1_8, i1_8 = v1_16[0:8], i1_16[0:8]
    cand = [tv[0][0] + v1_16]
    cexp = [ti[0][0] * N_KEYS + i1_16]
    for a in range(1, 8):
        cand.append(tv[0][a] + v1_8)
        cexp.append(ti[0][a] * N_KEYS + i1_8)
    cand.append(jnp.concatenate(tv[0][8:16], axis=0) + tv[1][0])
    cexp.append(jnp.concatenate(ti[0][8:16], axis=0) * N_KEYS + ti[1][0])
    cand = jnp.concatenate(cand, axis=0)
    cexp = jnp.concatenate(cexp, axis=0)
    cflat = cflat_ref[...]
    real = cflat >= 0.0
    cand = jnp.where(real, cand, -jnp.inf)
    big = float(PEER_TOPK * PEER_TOPK)

    best, eids = [], []
    for _ in range(PEER_TOPK):
        m = jnp.max(cand, axis=0, keepdims=True)
        pos = jnp.min(jnp.where((cand == m) & real, cflat, big), axis=0, keepdims=True)
        sel = cflat == pos
        eids.append(jnp.max(jnp.where(sel, cexp, -1.0), axis=0, keepdims=True))
        best.append(m)
        cand = jnp.where(sel, -jnp.inf, cand)
    best = jnp.concatenate(best, axis=0)
    ex = jnp.exp(best - best[0:1])
    gate_ref[...] = ex * (1.0 / jnp.sum(ex, axis=0, keepdims=True))
    eidx_ref[...] = jnp.concatenate(eids, axis=0).astype(jnp.int32)


def _topk(qp, sk_bf, cflat_b, tm=512):
    t = qp.shape[0]
    return pl.pallas_call(
        _topk_kernel,
        out_shape=(jax.ShapeDtypeStruct((PICKS, t), jnp.int32),
                   jax.ShapeDtypeStruct((PICKS, t), F32)),
        grid=(t // tm, PEER_HEADS),
        in_specs=[
            pl.BlockSpec((tm, 2 * N_KEYS), lambda i, p: (i, p)),
            pl.BlockSpec((2, N_KEYS, N_KEYS), lambda i, p: (0, 0, 0)),
            pl.BlockSpec((_CAND_ROWS, tm), lambda i, p: (0, 0)),
        ],
        out_specs=(pl.BlockSpec((PEER_TOPK, tm), lambda i, p: (p, i)),
                   pl.BlockSpec((PEER_TOPK, tm), lambda i, p: (p, i))),
        compiler_params=pltpu.CompilerParams(
            dimension_semantics=("parallel", "parallel"), vmem_limit_bytes=VMEM_LIMIT),
        name="peer_topk",
    )(qp, sk_bf, cflat_b)


_HALF = D_MODEL // 2
_HROWS = _HALF // LANES
_TT = 128
_GRP = 8
_NGRP = _TT // _GRP


def _gather_group(off_ref, tab_ref, stg, g):
    for j in range(_GRP):
        row = off_ref.at[pl.ds(pl.multiple_of((g * _GRP + j) * PICKS, PICKS), PICKS)]
        for k in range(PICKS):
            src = tab_ref[pl.ds(pl.multiple_of(row[k], _HROWS), _HROWS), :]
            stg[j][k * _HROWS:(k + 1) * _HROWS, :] = src


def _staged_matrix(stg):
    cols = [pltpu.bitcast(stg[pl.ds(s, PICKS, stride=_HROWS), :], BF16) for s in range(_HROWS)]
    return jnp.concatenate(cols, axis=1)


def _split_bf16(x):
    hi = x.astype(BF16)
    lo = (x - hi.astype(F32)).astype(BF16)
    return hi, lo


def _swap_pairs(x):
    even = (lax.broadcasted_iota(jnp.int32, (1, LANES), 1) & 1) == 0
    out = []
    for c in range(x.shape[1] // LANES):
        blk = x[:, c * LANES:(c + 1) * LANES]
        out.append(jnp.where(even, pltpu.roll(blk, LANES - 1, 1), pltpu.roll(blk, 1, 1)))
    return jnp.concatenate(out, axis=1)


def _peer_act_kernel(idx_ref, x_ref, tab_ref, gate_ref, w_ref, *stg):
    even = (lax.broadcasted_iota(jnp.int32, (1, 2 * PICKS), 1) & 1) == 0

    def gather_group(g):
        _gather_group(idx_ref, tab_ref, stg, g)

    def dot_group(g):
        r0 = pl.multiple_of(g * _GRP, _GRP)
        hi, lo = _split_bf16(x_ref[pl.ds(r0, _GRP), :])
        lhs = jnp.concatenate([hi[:, :_HALF], hi[:, _HALF:], lo[:, :_HALF], lo[:, _HALF:]], axis=0)
        tok = lax.broadcasted_iota(jnp.int32, lhs.shape, 0) & (_GRP - 1)
        r = jnp.zeros((4 * _GRP, 2 * PICKS), F32)
        for j in range(_GRP):
            r = r + _dot_nt(jnp.where(tok == j, lhs, jnp.zeros_like(lhs)), _staged_matrix(stg[j]))
        first = r[0:_GRP] + r[2 * _GRP:3 * _GRP]
        second = r[_GRP:2 * _GRP] + r[3 * _GRP:4 * _GRP]
        w_ref[pl.ds(r0, _GRP), :] = jnp.where(even, first, second)

    gather_group(0)

    def body(g, carry):
        dot_group(g - 1)
        gather_group(g)
        return carry

    lax.fori_loop(1, _NGRP, body, 0)
    dot_group(_NGRP - 1)

    part = w_ref[...]
    h = part + _swap_pairs(part)
    act = 0.5 * h * (1.0 + lax.erf(h * (2.0 ** -0.5)))
    w_ref[...] = gate_ref[...] * act


def _peer_out_kernel(idx_ref, w_ref, x2_ref, tab_ref, y_ref, *stg):
    even = (lax.broadcasted_iota(jnp.int32, (1, 2 * PICKS), 1) & 1) == 0

    def gather_group(g):
        _gather_group(idx_ref, tab_ref, stg, g)

    def dot_group(g):
        r0 = pl.multiple_of(g * _GRP, _GRP)
        hi, lo = _split_bf16(w_ref[pl.ds(r0, _GRP), :])
        zero = jnp.zeros_like(hi)
        lhs = jnp.concatenate([jnp.where(even, hi, zero), jnp.where(even, zero, hi),
                               jnp.where(even, lo, zero), jnp.where(even, zero, lo)], axis=0)
        tok = lax.broadcasted_iota(jnp.int32, lhs.shape, 0) & (_GRP - 1)
        r = jnp.zeros((4 * _GRP, _HALF), F32)
        for j in range(_GRP):
            r = r + jnp.dot(jnp.where(tok == j, lhs, jnp.zeros_like(lhs)), _staged_matrix(stg[j]),
                            preferred_element_type=F32)
        upd = jnp.concatenate([r[0:_GRP] + r[2 * _GRP:3 * _GRP], r[_GRP:2 * _GRP] + r[3 * _GRP:4 * _GRP]], axis=1)
        y_ref[pl.ds(r0, _GRP), :] = x2_ref[pl.ds(r0, _GRP), :] + upd

    gather_group(0)

    def body(g, carry):
        dot_group(g - 1)
        gather_group(g)
        return carry

    lax.fori_loop(1, _NGRP, body, 0)
    dot_group(_NGRP - 1)


def _table_spec(rows):
    return pl.BlockSpec((rows, LANES), lambda i: (0, 0), pipeline_mode=pl.Buffered(1))


def _staging():
    return [pltpu.VMEM((PICKS * _HROWS, LANES), jnp.uint32) for _ in range(_GRP)]


def _peer_act(off_flat, hn, tab, gate2):
    t = hn.shape[0]
    return pl.pallas_call(
        _peer_act_kernel,
        out_shape=jax.ShapeDtypeStruct((t, 2 * PICKS), F32),
        grid=(t // _TT,),
        in_specs=[
            pl.BlockSpec((_TT * PICKS,), lambda i: (i,), memory_space=pltpu.SMEM),
            pl.BlockSpec((_TT, D_MODEL), lambda i: (i, 0)),
            _table_spec(tab.shape[0]),
            pl.BlockSpec((_TT, 2 * PICKS), lambda i: (i, 0)),
        ],
        out_specs=pl.BlockSpec((_TT, 2 * PICKS), lambda i: (i, 0)),
        scratch_shapes=_staging(),
        compiler_params=pltpu.CompilerParams(
            dimension_semantics=("parallel",), vmem_limit_bytes=VMEM_LIMIT),
        name="peer_act",
    )(off_flat, hn, tab, gate2)


def _peer_out(off_flat, w2, x2, tab):
    t = x2.shape[0]
    return pl.pallas_call(
        _peer_out_kernel,
        out_shape=jax.ShapeDtypeStruct(x2.shape, F32),
        grid=(t // _TT,),
        in_specs=[
            pl.BlockSpec((_TT * PICKS,), lambda i: (i,), memory_space=pltpu.SMEM),
            pl.BlockSpec((_TT, 2 * PICKS), lambda i: (i, 0)),
            pl.BlockSpec((_TT, D_MODEL), lambda i: (i, 0)),
            _table_spec(tab.shape[0]),
        ],
        out_specs=pl.BlockSpec((_TT, D_MODEL), lambda i: (i, 0)),
        scratch_shapes=_staging(),
        compiler_params=pltpu.CompilerParams(
            dimension_semantics=("parallel",), vmem_limit_bytes=VMEM_LIMIT),
        name="peer_out",
    )(off_flat, w2, x2, tab)


def _rel_bucket(rel):
    nb = NUM_BUCKETS // 2
    max_exact = nb // 2
    n = jnp.abs(rel)
    large = max_exact + (jnp.log(jnp.maximum(n, 1).astype(F32) / max_exact)
                         / math.log(MAX_DISTANCE / max_exact) * (nb - max_exact)).astype(jnp.int32)
    large = jnp.minimum(large, nb - 1)
    return (rel > 0).astype(jnp.int32) * nb + jnp.where(n < max_exact, n, large)


def _bucket_saturation():
    nb = NUM_BUCKETS // 2
    max_exact = nb // 2
    n = np.arange(1, 8192, dtype=np.float64)
    large = max_exact + np.floor(np.log(n / max_exact) / math.log(MAX_DISTANCE / max_exact) * (nb - max_exact))
    return int(n[np.argmax(large >= nb - 1)])


def _toeplitz(v, nq, nk, k0):
    h = v.shape[0]
    width = nq + nk - 1
    w = v[:, k0 - (nq - 1):k0 + nk]
    w = jnp.concatenate([w, jnp.zeros((h, 1), v.dtype)], axis=1)
    rows = jnp.tile(w, (1, nq))[:, :nq * width].reshape(h, nq, width)
    return rows[:, :, nq - 1:nq - 1 + nk]


def _dilated_bias_tiles(bias_a):
    reach = _QT + BAND_HALF
    off = np.arange(-reach, reach + 1)
    tiles = []
    for d in DILATIONS:
        v = bias_a[_rel_bucket(jnp.asarray(off * d, jnp.int32))].astype(F32)
        v = jnp.where(jnp.asarray(np.abs(off) <= BAND_HALF)[:, None], v, NEG_INF)
        tiles.append(_toeplitz(v.T, _QT, _KT, reach - BAND_HALF))
    return jnp.stack(tiles, 0)


def _diff_bias_tiles(bias_b):
    assert (_DB - 1) * _DQ + 1 >= _bucket_saturation() + _DQ // 2
    reach = (_DB + 1) * _DQ
    rel = np.arange(-reach, reach + 1)
    v = bias_b[_rel_bucket(jnp.asarray(rel, jnp.int32))].astype(F32).T
    tiles = [_toeplitz(v, _DQ, _DQ, reach + delta * _DQ) for delta in range(-_DB, _DB + 1)]
    return jnp.stack(tiles, 1)


def _pack_table(w):
    e = w.shape[0]
    bits = lax.bitcast_convert_type(w.astype(BF16), jnp.uint16).astype(jnp.uint32)
    packed = bits[:, :_HALF] | (bits[:, _HALF:] << 16)
    return packed.reshape(e * _HROWS, LANES)


def _tile_gain(g, scale=1.0):
    return jnp.tile(g.astype(F32) * scale, SEG // HEAD_DIM).reshape(1, SEG)


def kernel(x_prompt, x_sample, rel_bias, attn_norm_g, w_in, q_norm_a, k_norm_a, q_norm_b, k_norm_b,
           lambda_q1, lambda_k1, lambda_q2, lambda_k2, diff_norm_g, w_out, ffn_norm_g,
           peer_w_q, peer_sub_keys, peer_u, peer_v):
    gsum = jnp.asarray(np.kron(np.eye(SEG // HEAD_DIM), np.ones((HEAD_DIM, HEAD_DIM))), BF16)
    gains = jnp.stack([_tile_gain(q_norm_a[0], ATTN_SCALE), _tile_gain(k_norm_a[0]),
                       _tile_gain(q_norm_b[0], ATTN_SCALE), _tile_gain(k_norm_b[0])], 0)
    attn_g = attn_norm_g[0].reshape(1, D_MODEL).astype(F32)
    ffn_g = ffn_norm_g[0].reshape(1, D_MODEL).astype(F32)
    w_in_bf, w_out_bf, w_q_bf = w_in[0].astype(BF16), w_out[0].astype(BF16), peer_w_q[0].astype(BF16)
    sk_bf = peer_sub_keys[0].astype(BF16)
    bias_a = _dilated_bias_tiles(rel_bias[:, :N_HEADS_A])
    bias_b = _diff_bias_tiles(rel_bias[:, N_HEADS_A:])
    lam = (jnp.exp(jnp.sum(lambda_q1[0].astype(F32) * lambda_k1[0].astype(F32)))
           - jnp.exp(jnp.sum(lambda_q2[0].astype(F32) * lambda_k2[0].astype(F32)))
           + LAMBDA_INIT).reshape(1)
    dgain = diff_norm_g[0].astype(F32).reshape(1, LANES)
    tab_u, tab_v = _pack_table(peer_u[0]), _pack_table(peer_v[0])
    tm = 512
    cflat_b = jnp.asarray(np.broadcast_to(_cand_layout()[:, None], (_CAND_ROWS, tm)).copy())

    def encode(x3):
        batch, seq, _ = x3.shape
        t = batch * seq
        x = x3.reshape(t, D_MODEL)
        proj3 = _in_proj(x, attn_g, w_in_bf, gsum, gains).reshape(batch, seq, IN_WIDTH)
        oa = _dilated_attention(proj3, bias_a, batch, seq).reshape(t, SEG)
        ob = _diff_attention(lam, proj3, bias_b, dgain, batch, seq).reshape(t, SEG)
        x2, hn, qp = _out_proj(x, oa, ob, w_out_bf, ffn_g, w_q_bf)
        eidx_t, gate_t = _topk(qp, sk_bf, cflat_b, tm)
        off_flat = (eidx_t.T * _HROWS).reshape(t * PICKS)
        gate2 = jnp.repeat(gate_t.T, 2, axis=1)
        w2 = _peer_act(off_flat, hn, tab_u, gate2)
        return _peer_out(off_flat, w2, x2, tab_v).reshape(batch, seq, D_MODEL)

    return (encode(x_prompt), encode(x_sample))
```

```python
import functools
import math

import numpy as np
import jax
import jax.numpy as jnp
from jax import lax
from jax.experimental import pallas as pl
from jax.experimental.pallas import tpu as pltpu

F32 = jnp.float32
BF16 = jnp.bfloat16

D_MODEL = 1024
HEAD_DIM = 64
N_HEADS_A = 8
N_HEADS_B = 4
SEG = 512
IN_WIDTH = 6 * SEG
LANES = 128
DILATIONS = (1, 4, 16)
BAND_HALF = 64
NUM_BUCKETS = 32
MAX_DISTANCE = 1024
N_KEYS = 128
PEER_HEADS = 8
PEER_TOPK = 16
PICKS = PEER_HEADS * PEER_TOPK
NORM_EPS = 1e-6
NEG_INF = -1e30
ATTN_SCALE = HEAD_DIM ** -0.5
LAMBDA_INIT = 0.8 - 0.6 * math.exp(-0.3 * 0)

VMEM_LIMIT = 48 * 1024 * 1024


def _dot_nt(a, b):
    return lax.dot_general(a, b, (((1,), (1,)), ((), ())), preferred_element_type=F32)


def _in_proj_kernel(x_ref, g_ref, w_ref, gsum_ref, gains_ref, o_ref):
    x = x_ref[...]
    ms = jnp.mean(x * x, axis=-1, keepdims=True)
    xn = ((x * lax.rsqrt(ms + NORM_EPS)) * g_ref[...]).astype(BF16)
    gi = 0
    for seg in range(6):
        acc = jnp.dot(xn, w_ref[:, seg * SEG:(seg + 1) * SEG], preferred_element_type=F32)
        if seg in (0, 1, 3, 4):
            sq = acc * acc
            hi = sq.astype(BF16)
            lo = (sq - hi.astype(F32)).astype(BF16)
            gs = (jnp.dot(hi, gsum_ref[...], preferred_element_type=F32)
                  + jnp.dot(lo, gsum_ref[...], preferred_element_type=F32))
            acc = (acc * lax.rsqrt(gs * (1.0 / HEAD_DIM) + NORM_EPS)) * gains_ref[gi]
            gi += 1
        o_ref[:, seg * SEG:(seg + 1) * SEG] = acc.astype(BF16)


def _in_proj(x, g, w_bf, gsum, gains, tm=512):
    t = x.shape[0]
    return pl.pallas_call(
        _in_proj_kernel,
        out_shape=jax.ShapeDtypeStruct((t, IN_WIDTH), BF16),
        grid=(t // tm,),
        in_specs=[
            pl.BlockSpec((tm, D_MODEL), lambda i: (i, 0)),
            pl.BlockSpec((1, D_MODEL), lambda i: (0, 0)),
            pl.BlockSpec((D_MODEL, IN_WIDTH), lambda i: (0, 0)),
            pl.BlockSpec((SEG, SEG), lambda i: (0, 0)),
            pl.BlockSpec((4, 1, SEG), lambda i: (0, 0, 0)),
        ],
        out_specs=pl.BlockSpec((tm, IN_WIDTH), lambda i: (i, 0)),
        compiler_params=pltpu.CompilerParams(
            dimension_semantics=("parallel",), vmem_limit_bytes=VMEM_LIMIT),
        name="in_proj",
    )(x, g, w_bf, gsum, gains)


_QT = 128
_KT = _QT + 2 * BAND_HALF
_PAD = BAND_HALF * max(DILATIONS)


def _dilated_kernel(q_ref, k_ref, v_ref, bias_ref, o_ref,
                    qf, kf, vf, qc, kc, vc, ob, lb, *, seq):
    s = seq
    lane = lax.broadcasted_iota(jnp.int32, (1, LANES), 1)
    head0 = lane < HEAD_DIM

    qf[...] = q_ref[...].astype(F32)
    zpad = jnp.zeros((_PAD, LANES), F32)
    kf[0:_PAD, :] = zpad
    kf[_PAD + s:_PAD + s + _PAD, :] = zpad
    vf[0:_PAD, :] = zpad
    vf[_PAD + s:_PAD + s + _PAD, :] = zpad
    kf[_PAD:_PAD + s, :] = k_ref[...].astype(F32)
    vf[_PAD:_PAD + s, :] = v_ref[...].astype(F32)

    for br, d in enumerate(DILATIONS):
        n = s // d
        nt = n // _QT
        for c in range(d):
            if d == 1:
                qc[0:n, :] = q_ref[...]
                kc[0:n + 2 * BAND_HALF, :] = kf[_PAD - BAND_HALF:_PAD + s + BAND_HALF, :].astype(BF16)
                vc[0:n + 2 * BAND_HALF, :] = vf[_PAD - BAND_HALF:_PAD + s + BAND_HALF, :].astype(BF16)
            else:
                k0 = _PAD + c - BAND_HALF * d
                qc[0:n, :] = qf[pl.ds(c, n, stride=d), :].astype(BF16)
                kc[0:n + 2 * BAND_HALF, :] = kf[pl.ds(k0, n + 2 * BAND_HALF, stride=d), :].astype(BF16)
                vc[0:n + 2 * BAND_HALF, :] = vf[pl.ds(k0, n + 2 * BAND_HALF, stride=d), :].astype(BF16)

            def tile(t, br=br, d=d, c=c, n=n):
                q0 = t * _QT if isinstance(t, int) else pl.multiple_of(t * _QT, _QT)
                qt = qc[pl.ds(q0, _QT), :]
                kt = kc[pl.ds(q0, _KT), :]
                vt = vc[pl.ds(q0, _KT), :]
                kpos = lax.broadcasted_iota(jnp.int32, (1, _KT), 1) + (q0 - BAND_HALF)
                valid = (kpos >= 0) & (kpos < n)
                outs, lses = [], []
                for h in range(2):
                    qh = jnp.where(head0 if h == 0 else jnp.logical_not(head0), qt, jnp.zeros_like(qt))
                    sc = _dot_nt(qh, kt) + bias_ref[br, h]
                    sc = jnp.where(valid, sc, NEG_INF)
                    m = jnp.max(sc, axis=-1, keepdims=True)
                    p = jnp.exp(sc - m)
                    l = jnp.sum(p, axis=-1, keepdims=True)
                    o = jnp.dot(p.astype(BF16), vt, preferred_element_type=F32)
                    outs.append(o * (1.0 / l))
                    lses.append(m + jnp.log(l))
                o = jnp.where(head0, outs[0], outs[1])
                lse = jnp.where(head0, lses[0], lses[1])
                if d == 1:
                    ob[br, pl.ds(q0, _QT), :] = o
                    lb[br, pl.ds(q0, _QT), :] = lse
                else:
                    ob[br, pl.ds(c + q0 * d, _QT, stride=d), :] = o
                    lb[br, pl.ds(c + q0 * d, _QT, stride=d), :] = lse

            per_iter = 4 if nt % 4 == 0 else (2 if nt % 2 == 0 else 1)

            def tile_group(tg, carry, tile=tile, per_iter=per_iter):
                for u in range(per_iter):
                    tile(per_iter * tg + u)
                return carry

            if nt == per_iter:
                for u in range(nt):
                    tile(u)
            else:
                lax.fori_loop(0, nt // per_iter, tile_group, 0)

    rows = 512
    for r in range(s // rows):
        sl = slice(r * rows, (r + 1) * rows)
        l0, l1, l2 = lb[0, sl, :], lb[1, sl, :], lb[2, sl, :]
        mx = jnp.maximum(jnp.maximum(l0, l1), l2)
        e0, e1, e2 = jnp.exp(l0 - mx), jnp.exp(l1 - mx), jnp.exp(l2 - mx)
        inv = 1.0 / (e0 + e1 + e2)
        out = (e0 * inv) * ob[0, sl, :] + (e1 * inv) * ob[1, sl, :] + (e2 * inv) * ob[2, sl, :]
        o_ref[sl, :] = out.astype(BF16)


def _dilated_attention(proj3, bias_tiles, batch, seq):
    nblk = SEG // LANES
    return pl.pallas_call(
        functools.partial(_dilated_kernel, seq=seq),
        out_shape=jax.ShapeDtypeStruct((batch, seq, SEG), BF16),
        grid=(batch, nblk),
        in_specs=[
            pl.BlockSpec((None, seq, LANES), lambda b, h: (b,0, h)),
            pl.BlockSpec((None, seq, LANES), lambda b, h: (b,0, nblk + h)),
            pl.BlockSpec((None, seq, LANES), lambda b, h: (b,0, 2 * nblk + h)),
            pl.BlockSpec((3, 2, _QT, _KT), lambda b, h: (0, h, 0, 0)),
        ],
        out_specs=pl.BlockSpec((None, seq, LANES), lambda b, h: (b, 0, h)),
        scratch_shapes=[
            pltpu.VMEM((seq, LANES), F32),
            pltpu.VMEM((seq + 2 * _PAD, LANES), F32),
            pltpu.VMEM((seq + 2 * _PAD, LANES), F32),
            pltpu.VMEM((seq, LANES), BF16),
            pltpu.VMEM((seq + 2 * BAND_HALF, LANES), BF16),
            pltpu.VMEM((seq + 2 * BAND_HALF, LANES), BF16),
            pltpu.VMEM((3, seq, LANES), F32),
            pltpu.VMEM((3, seq, LANES), F32),
        ],
        compiler_params=pltpu.CompilerParams(
            dimension_semantics=("parallel", "parallel"), vmem_limit_bytes=VMEM_LIMIT),
        name="dilated_attn",
    )(proj3, proj3, proj3, bias_tiles)


_DQ = 256
_DB = 4
_DCH = 8


def _diff_kernel(lam_ref, q_ref, k_ref, v_ref, bias_ref, g_ref, o_ref,
                 s1_sc, s2_sc, m1_sc, m2_sc, l1_sc, l2_sc, a1_sc, a2_sc, *, nc):
    i = pl.program_id(2)
    lane = lax.broadcasted_iota(jnp.int32, (1, LANES), 1)
    first = lane < HEAD_DIM
    q = q_ref[...]
    q1 = jnp.where(first, q, jnp.zeros_like(q))
    q2 = jnp.where(first, jnp.zeros_like(q), q)

    m1_sc[...] = jnp.full((_DQ, LANES), -jnp.inf, F32)
    m2_sc[...] = jnp.full((_DQ, LANES), -jnp.inf, F32)
    zero = jnp.zeros((_DQ, LANES), F32)
    l1_sc[...] = zero
    l2_sc[...] = zero
    a1_sc[...] = zero
    a2_sc[...] = zero

    def lane_fold(x, op):
        parts = [x[:, c * LANES:(c + 1) * LANES] for c in range(x.shape[1] // LANES)]
        while len(parts) > 1:
            parts = [op(parts[a], parts[a + 1]) for a in range(0, len(parts), 2)]
        return parts[0]

    width = _DCH * _DQ

    def score_body(jj, carry):
        j = _DCH * jj
        k0 = pl.multiple_of(j * _DQ, width)
        kj = k_ref[pl.ds(k0, width), :]
        b = jnp.concatenate([bias_ref[jnp.clip(j + u - i, -_DB, _DB) + _DB] for u in range(_DCH)], axis=1)
        s1 = _dot_nt(q1, kj) + b
        s2 = _dot_nt(q2, kj) + b
        s1_sc[jj] = s1
        s2_sc[jj] = s2
        m1_sc[...] = jnp.maximum(m1_sc[...], lane_fold(s1, jnp.maximum))
        m2_sc[...] = jnp.maximum(m2_sc[...], lane_fold(s2, jnp.maximum))
        return carry

    lax.fori_loop(0, nc // _DCH, score_body, 0)
    m1 = jnp.max(m1_sc[...], axis=-1, keepdims=True)
    m2 = jnp.max(m2_sc[...], axis=-1, keepdims=True)

    def pv_body(jj, carry):
        k0 = pl.multiple_of(jj * width, width)
        vj = v_ref[pl.ds(k0, width), :]
        p1 = jnp.exp(s1_sc[jj] - m1)
        p2 = jnp.exp(s2_sc[jj] - m2)
        l1_sc[...] += lane_fold(p1, jnp.add)
        l2_sc[...] += lane_fold(p2, jnp.add)
        a1_sc[...] += jnp.dot(p1.astype(BF16), vj, preferred_element_type=F32)
        a2_sc[...] += jnp.dot(p2.astype(BF16), vj, preferred_element_type=F32)
        return carry

    lax.fori_loop(0, nc // _DCH, pv_body, 0)
    l1 = jnp.sum(l1_sc[...], axis=-1, keepdims=True)
    l2 = jnp.sum(l2_sc[...], axis=-1, keepdims=True)
    lam = lam_ref[0]
    o = a1_sc[...] * (1.0 / l1) - (lam * a2_sc[...]) * (1.0 / l2)
    ms = jnp.mean(o * o, axis=-1, keepdims=True)
    o = ((o * lax.rsqrt(ms + NORM_EPS)) * g_ref[...]) * (1.0 - LAMBDA_INIT)
    o_ref[...] = o.astype(BF16)


def _diff_attention(lam, proj3, bias_tiles, gain, batch, seq):
    nblk = SEG // LANES
    nc = seq // _DQ
    base = 3 * nblk
    return pl.pallas_call(
        functools.partial(_diff_kernel, nc=nc),
        out_shape=jax.ShapeDtypeStruct((batch, seq, SEG), BF16),
        grid=(batch, N_HEADS_B, nc),
        in_specs=[
            pl.BlockSpec(memory_space=pltpu.SMEM),
            pl.BlockSpec((None, _DQ, LANES), lambda b, h, i: (b,i, base + h)),
            pl.BlockSpec((None, seq, LANES), lambda b, h, i: (b,0, base + nblk + h)),
            pl.BlockSpec((None, seq, LANES), lambda b, h, i: (b,0, base + 2 * nblk + h)),
            pl.BlockSpec((None, 2 * _DB + 1, _DQ, _DQ), lambda b, h, i: (h, 0, 0, 0)),
            pl.BlockSpec((1, LANES), lambda b, h, i: (0, 0)),
        ],
        out_specs=pl.BlockSpec((None, _DQ, LANES), lambda b, h, i: (b, i, h)),
        scratch_shapes=[
            pltpu.VMEM((nc // _DCH, _DQ, _DCH * _DQ), F32),
            pltpu.VMEM((nc // _DCH, _DQ, _DCH * _DQ), F32),
            pltpu.VMEM((_DQ, LANES), F32),
            pltpu.VMEM((_DQ, LANES), F32),
            pltpu.VMEM((_DQ, LANES), F32),
            pltpu.VMEM((_DQ, LANES), F32),
            pltpu.VMEM((_DQ, LANES), F32),
            pltpu.VMEM((_DQ, LANES), F32),
        ],
        compiler_params=pltpu.CompilerParams(
            dimension_semantics=("parallel", "parallel", "arbitrary"), vmem_limit_bytes=VMEM_LIMIT),
        name="diff_attn",
    )(lam, proj3, proj3, proj3, bias_tiles, gain)


def _out_proj_kernel(x_ref, oa_ref, ob_ref, wo_ref, g_ref, wq_ref, x2_ref, hn_ref, qp_ref):
    x2 = (x_ref[...]
          + jnp.dot(oa_ref[...], wo_ref[0:SEG, :], preferred_element_type=F32)
          + jnp.dot(ob_ref[...], wo_ref[SEG:2 * SEG, :], preferred_element_type=F32))
    x2_ref[...] = x2
    ms = jnp.mean(x2 * x2, axis=-1, keepdims=True)
    hn = (x2 * lax.rsqrt(ms + NORM_EPS)) * g_ref[...]
    hn_ref[...] = hn
    qp_ref[...] = jnp.dot(hn.astype(BF16), wq_ref[...], preferred_element_type=F32).astype(BF16)


def _out_proj(x, oa, ob, wo_bf, g, wq_bf, tm=512):
    t = x.shape[0]
    qw = wq_bf.shape[1]
    return pl.pallas_call(
        _out_proj_kernel,
        out_shape=(jax.ShapeDtypeStruct((t, D_MODEL), F32),
                   jax.ShapeDtypeStruct((t, D_MODEL), F32),
                   jax.ShapeDtypeStruct((t, qw), BF16)),
        grid=(t // tm,),
        in_specs=[
            pl.BlockSpec((tm, D_MODEL), lambda i: (i, 0)),
            pl.BlockSpec((tm, SEG), lambda i: (i, 0)),
            pl.BlockSpec((tm, SEG), lambda i: (i, 0)),
            pl.BlockSpec((2 * SEG, D_MODEL), lambda i: (0, 0)),
            pl.BlockSpec((1, D_MODEL), lambda i: (0, 0)),
            pl.BlockSpec((D_MODEL, qw), lambda i: (0, 0)),
        ],
        out_specs=(pl.BlockSpec((tm, D_MODEL), lambda i: (i, 0)),
                   pl.BlockSpec((tm, D_MODEL), lambda i: (i, 0)),
                   pl.BlockSpec((tm, qw), lambda i: (i, 0))),
        compiler_params=pltpu.CompilerParams(
            dimension_semantics=("parallel",), vmem_limit_bytes=VMEM_LIMIT),
        name="out_proj",
    )(x, oa, ob, wo_bf, g, wq_bf)


_CAND_ROWS = 16 + 7 * 8 + 8


def _cand_layout():
    flat = np.full((_CAND_ROWS,), -1, np.float32)
    flat[0:16] = np.arange(16)
    for a in range(1, 8):
        nb = PEER_TOPK // (a + 1)
        flat[16 + (a - 1) * 8:16 + (a - 1) * 8 + nb] = a * PEER_TOPK + np.arange(nb)
    flat[72:80] = np.arange(8, 16) * PEER_TOPK
    return flat


def _topk_kernel(qp_ref, sk_ref, cflat_ref, eidx_ref, gate_ref):
    tm = qp_ref.shape[0]
    kiota = lax.broadcasted_iota(jnp.int32, (N_KEYS, tm), 0).astype(F32)
    tv, ti = [], []
    for z in range(2):
        sc = _dot_nt(sk_ref[z], qp_ref[:, z * N_KEYS:(z + 1) * N_KEYS])
        vals, idxs = [], []
        for _ in range(PEER_TOPK):
            m = jnp.max(sc, axis=0, keepdims=True)
            idx = jnp.min(jnp.where(sc == m, kiota, float(N_KEYS)), axis=0, keepdims=True)
            vals.append(m)
            idxs.append(idx)
            sc = jnp.where(kiota == idx, -jnp.inf, sc)
        tv.append(vals)
        ti.append(idxs)

    v1_16 = jnp.concatenate(tv[1], axis=0)
    i1_16 = jnp.concatenate(ti[1], axis=0)
    v1_8, i1_8 = v1_16[0:8], i1_16[0:8]
    cand = [tv[0][0] + v1_16]
    cexp = [ti[0][0] * N_KEYS + i1_16]
    for a in range(1, 8):
        cand.append(tv[0][a] + v1_8)
        cexp.append(ti[0][a] * N_KEYS + i1_8)
    cand.append(jnp.concatenate(tv[0][8:16], axis=0) + tv[1][0])
    cexp.append(jnp.concatenate(ti[0][8:16], axis=0) * N_KEYS + ti[1][0])
    cand = jnp.concatenate(cand, axis=0)
    cexp = jnp.concatenate(cexp, axis=0)
    cflat = cflat_ref[...]
    real = cflat >= 0.0
    cand = jnp.where(real, cand, -jnp.inf)
    big = float(PEER_TOPK * PEER_TOPK)

    best, eids = [], []
    for _ in range(PEER_TOPK):
        m = jnp.max(cand, axis=0, keepdims=True)
        pos = jnp.min(jnp.where((cand == m) & real, cflat, big), axis=0, keepdims=True)
        sel = cflat == pos
        eids.append(jnp.max(jnp.where(sel, cexp, -1.0), axis=0, keepdims=True))
        best.append(m)
        cand = jnp.where(sel, -jnp.inf, cand)
    best = jnp.concatenate(best, axis=0)
    ex = jnp.exp(best - best[0:1])
    gate_ref[...] = ex * (1.0 / jnp.sum(ex, axis=0, keepdims=True))
    eidx_ref[...] = jnp.concatenate(eids, axis=0).astype(jnp.int32)


def _topk(qp, sk_bf, cflat_b, tm=512):
    t = qp.shape[0]
    return pl.pallas_call(
        _topk_kernel,
        out_shape=(jax.ShapeDtypeStruct((PICKS, t), jnp.int32),
                   jax.ShapeDtypeStruct((PICKS, t), F32)),
        grid=(t // tm, PEER_HEADS),
        in_specs=[
            pl.BlockSpec((tm, 2 * N_KEYS), lambda i, p: (i, p)),
            pl.BlockSpec((2, N_KEYS, N_KEYS), lambda i, p: (0, 0, 0)),
            pl.BlockSpec((_CAND_ROWS, tm), lambda i, p: (0, 0)),
        ],
        out_specs=(pl.BlockSpec((PEER_TOPK, tm), lambda i, p: (p, i)),
                   pl.BlockSpec((PEER_TOPK, tm), lambda i, p: (p, i))),
        compiler_params=pltpu.CompilerParams(
            dimension_semantics=("parallel", "parallel"), vmem_limit_bytes=VMEM_LIMIT),
        name="peer_topk",
    )(qp, sk_bf, cflat_b)


_HALF = D_MODEL // 2
_HROWS = _HALF // LANES
_TT = 128
_GRP = 8
_NGRP = _TT // _GRP
_OFFS = 16


def _gather_group(off_ref, tab_ref, stg, g):
    for j in range(_GRP):
        for c in range(PICKS // _OFFS):
            row = off_ref.at[pl.ds(pl.multiple_of((g * _GRP + j) * PICKS + c * _OFFS, _OFFS), _OFFS)]
            for kk in range(0, _OFFS, 2):
                k = c * _OFFS + kk
                pair = [tab_ref[pl.ds(pl.multiple_of(row[kk + u], _HROWS), _HROWS), :] for u in range(2)]
                stg[j][k * _HROWS:(k + 2) * _HROWS, :] = jnp.concatenate(pair, axis=0)


def _staged_matrix(stg):
    cols = [pltpu.bitcast(stg[pl.ds(s, PICKS, stride=_HROWS), :], BF16) for s in range(_HROWS)]
    return jnp.concatenate(cols, axis=1)


def _split_bf16(x):
    hi = x.astype(BF16)
    lo = (x - hi.astype(F32)).astype(BF16)
    return hi, lo


def _swap_pairs(x):
    even = (lax.broadcasted_iota(jnp.int32, (1, LANES), 1) & 1) == 0
    out = []
    for c in range(x.shape[1] // LANES):
        blk = x[:, c * LANES:(c + 1) * LANES]
        out.append(jnp.where(even, pltpu.roll(blk, LANES - 1, 1), pltpu.roll(blk, 1, 1)))
    return jnp.concatenate(out, axis=1)


def _peer_act_kernel(idx_ref, x_ref, tab_ref, gate_ref, w_ref, *stg):
    even = (lax.broadcasted_iota(jnp.int32, (1, 2 * PICKS), 1) & 1) == 0

    def gather_group(g):
        _gather_group(idx_ref, tab_ref, stg, g)

    def dot_group(g):
        r0 = pl.multiple_of(g * _GRP, _GRP)
        hi, lo = _split_bf16(x_ref[pl.ds(r0, _GRP), :])
        lhs = jnp.concatenate([hi[:, :_HALF], hi[:, _HALF:], lo[:, :_HALF], lo[:, _HALF:]], axis=0)
        tok = lax.broadcasted_iota(jnp.int32, lhs.shape, 0) & (_GRP - 1)
        r = jnp.zeros((4 * _GRP, 2 * PICKS), F32)
        for j in range(_GRP):
            r = r + _dot_nt(jnp.where(tok == j, lhs, jnp.zeros_like(lhs)), _staged_matrix(stg[j]))
        first = r[0:_GRP] + r[2 * _GRP:3 * _GRP]
        second = r[_GRP:2 * _GRP] + r[3 * _GRP:4 * _GRP]
        w_ref[pl.ds(r0, _GRP), :] = jnp.where(even, first, second)

    gather_group(0)

    def body(g, carry):
        dot_group(g - 1)
        gather_group(g)
        return carry

    lax.fori_loop(1, _NGRP, body, 0)
    dot_group(_NGRP - 1)

    part = w_ref[...]
    h = part + _swap_pairs(part)
    act = 0.5 * h * (1.0 + lax.erf(h * (2.0 ** -0.5)))
    w_ref[...] = gate_ref[...] * act


def _peer_out_kernel(idx_ref, w_ref, x2_ref, tab_ref, y_ref, *stg):
    even = (lax.broadcasted_iota(jnp.int32, (1, 2 * PICKS), 1) & 1) == 0

    def gather_group(g):
        _gather_group(idx_ref, tab_ref, stg, g)

    def dot_group(g):
        r0 = pl.multiple_of(g * _GRP, _GRP)
        hi, lo = _split_bf16(w_ref[pl.ds(r0, _GRP), :])
        zero = jnp.zeros_like(hi)
        lhs = jnp.concatenate([jnp.where(even, hi, zero), jnp.where(even, zero, hi),
                               jnp.where(even, lo, zero), jnp.where(even, zero, lo)], axis=0)
        tok = lax.broadcasted_iota(jnp.int32, lhs.shape, 0) & (_GRP - 1)
        r = jnp.zeros((4 * _GRP, _HALF), F32)
        for j in range(_GRP):
            r = r + jnp.dot(jnp.where(tok == j, lhs, jnp.zeros_like(lhs)), _staged_matrix(stg[j]),
                            preferred_element_type=F32)
        upd = jnp.concatenate([r[0:_GRP] + r[2 * _GRP:3 * _GRP], r[_GRP:2 * _GRP] + r[3 * _GRP:4 * _GRP]], axis=1)
        y_ref[pl.ds(r0, _GRP), :] = x2_ref[pl.ds(r0, _GRP), :] + upd

    gather_group(0)

    def body(g, carry):
        dot_group(g - 1)
        gather_group(g)
        return carry

    lax.fori_loop(1, _NGRP, body, 0)
    dot_group(_NGRP - 1)


def _table_spec(rows):
    return pl.BlockSpec((rows, LANES), lambda i: (0, 0), pipeline_mode=pl.Buffered(1))


def _staging():
    return [pltpu.VMEM((PICKS * _HROWS, LANES), jnp.uint32) for _ in range(_GRP)]


def _peer_act(off_flat, hn, tab, gate2):
    t = hn.shape[0]
    return pl.pallas_call(
        _peer_act_kernel,
        out_shape=jax.ShapeDtypeStruct((t, 2 * PICKS), F32),
        grid=(t // _TT,),
        in_specs=[
            pl.BlockSpec((_TT * PICKS,), lambda i: (i,), memory_space=pltpu.SMEM),
            pl.BlockSpec((_TT, D_MODEL), lambda i: (i, 0)),
            _table_spec(tab.shape[0]),
            pl.BlockSpec((_TT, 2 * PICKS), lambda i: (i, 0)),
        ],
        out_specs=pl.BlockSpec((_TT, 2 * PICKS), lambda i: (i, 0)),
        scratch_shapes=_staging(),
        compiler_params=pltpu.CompilerParams(
            dimension_semantics=("parallel",), vmem_limit_bytes=VMEM_LIMIT),
        name="peer_act",
    )(off_flat, hn, tab, gate2)


def _peer_out(off_flat, w2, x2, tab):
    t = x2.shape[0]
    return pl.pallas_call(
        _peer_out_kernel,
        out_shape=jax.ShapeDtypeStruct(x2.shape, F32),
        grid=(t // _TT,),
        in_specs=[
            pl.BlockSpec((_TT * PICKS,), lambda i: (i,), memory_space=pltpu.SMEM),
            pl.BlockSpec((_TT, 2 * PICKS), lambda i: (i, 0)),
            pl.BlockSpec((_TT, D_MODEL), lambda i: (i, 0)),
            _table_spec(tab.shape[0]),
        ],
        out_specs=pl.BlockSpec((_TT, D_MODEL), lambda i: (i, 0)),
        scratch_shapes=_staging(),
        compiler_params=pltpu.CompilerParams(
            dimension_semantics=("parallel",), vmem_limit_bytes=VMEM_LIMIT),
        name="peer_out",
    )(off_flat, w2, x2, tab)


def _rel_bucket(rel):
    nb = NUM_BUCKETS // 2
    max_exact = nb // 2
    n = jnp.abs(rel)
    large = max_exact + (jnp.log(jnp.maximum(n, 1).astype(F32) / max_exact)
                         / math.log(MAX_DISTANCE / max_exact) * (nb - max_exact)).astype(jnp.int32)
    large = jnp.minimum(large, nb - 1)
    return (rel > 0).astype(jnp.int32) * nb + jnp.where(n < max_exact, n, large)


def _bucket_saturation():
    nb = NUM_BUCKETS // 2
    max_exact = nb // 2
    n = np.arange(1, 8192, dtype=np.float64)
    large = max_exact + np.floor(np.log(n / max_exact) / math.log(MAX_DISTANCE / max_exact) * (nb - max_exact))
    return int(n[np.argmax(large >= nb - 1)])


def _toeplitz(v, nq, nk, k0):
    h = v.shape[0]
    width = nq + nk - 1
    w = v[:, k0 - (nq - 1):k0 + nk]
    w = jnp.concatenate([w, jnp.zeros((h, 1), v.dtype)], axis=1)
    rows = jnp.tile(w, (1, nq))[:, :nq * width].reshape(h, nq, width)
    return rows[:, :, nq - 1:nq - 1 + nk]


def _dilated_bias_tiles(bias_a):
    reach = _QT + BAND_HALF
    off = np.arange(-reach, reach + 1)
    tiles = []
    for d in DILATIONS:
        v = bias_a[_rel_bucket(jnp.asarray(off * d, jnp.int32))].astype(F32)
        v = jnp.where(jnp.asarray(np.abs(off) <= BAND_HALF)[:, None], v, NEG_INF)
        tiles.append(_toeplitz(v.T, _QT, _KT, reach - BAND_HALF))
    return jnp.stack(tiles, 0)


def _diff_bias_tiles(bias_b):
    assert (_DB - 1) * _DQ + 1 >= _bucket_saturation() + _DQ // 2
    reach = (_DB + 1) * _DQ
    rel = np.arange(-reach, reach + 1)
    v = bias_b[_rel_bucket(jnp.asarray(rel, jnp.int32))].astype(F32).T
    tiles = [_toeplitz(v, _DQ, _DQ, reach + delta * _DQ) for delta in range(-_DB, _DB + 1)]
    return jnp.stack(tiles, 1)


def _pack_table(w):
    e = w.shape[0]
    bits = lax.bitcast_convert_type(w.astype(BF16), jnp.uint16).astype(jnp.uint32)
    packed = bits[:, :_HALF] | (bits[:, _HALF:] << 16)
    return packed.reshape(e * _HROWS, LANES)


def _tile_gain(g, scale=1.0):
    return jnp.tile(g.astype(F32) * scale, SEG // HEAD_DIM).reshape(1, SEG)


def kernel(x_prompt, x_sample, rel_bias, attn_norm_g, w_in, q_norm_a, k_norm_a, q_norm_b, k_norm_b,
           lambda_q1, lambda_k1, lambda_q2, lambda_k2, diff_norm_g, w_out, ffn_norm_g,
           peer_w_q, peer_sub_keys, peer_u, peer_v):
    gsum = jnp.asarray(np.kron(np.eye(SEG // HEAD_DIM), np.ones((HEAD_DIM, HEAD_DIM))), BF16)
    gains = jnp.stack([_tile_gain(q_norm_a[0], ATTN_SCALE), _tile_gain(k_norm_a[0]),
                       _tile_gain(q_norm_b[0], ATTN_SCALE), _tile_gain(k_norm_b[0])], 0)
    attn_g = attn_norm_g[0].reshape(1, D_MODEL).astype(F32)
    ffn_g = ffn_norm_g[0].reshape(1, D_MODEL).astype(F32)
    w_in_bf, w_out_bf, w_q_bf = w_in[0].astype(BF16), w_out[0].astype(BF16), peer_w_q[0].astype(BF16)
    sk_bf = peer_sub_keys[0].astype(BF16)
    bias_a = _dilated_bias_tiles(rel_bias[:, :N_HEADS_A])
    bias_b = _diff_bias_tiles(rel_bias[:, N_HEADS_A:])
    lam = (jnp.exp(jnp.sum(lambda_q1[0].astype(F32) * lambda_k1[0].astype(F32)))
           - jnp.exp(jnp.sum(lambda_q2[0].astype(F32) * lambda_k2[0].astype(F32)))
           + LAMBDA_INIT).reshape(1)
    dgain = diff_norm_g[0].astype(F32).reshape(1, LANES)
    tab_u, tab_v = _pack_table(peer_u[0]), _pack_table(peer_v[0])
    tm = 512
    cflat_b = jnp.asarray(np.broadcast_to(_cand_layout()[:, None], (_CAND_ROWS, tm)).copy())

    def encode(x3):
        batch, seq, _ = x3.shape
        t = batch * seq
        x = x3.reshape(t, D_MODEL)
        proj3 = _in_proj(x, attn_g, w_in_bf, gsum, gains).reshape(batch, seq, IN_WIDTH)
        oa = _dilated_attention(proj3, bias_a, batch, seq).reshape(t, SEG)
        ob = _diff_attention(lam, proj3, bias_b, dgain, batch, seq).reshape(t, SEG)
        x2, hn, qp = _out_proj(x, oa, ob, w_out_bf, ffn_g, w_q_bf)
        eidx_t, gate_t = _topk(qp, sk_bf, cflat_b, tm)
        off_flat = (eidx_t.T * _HROWS).reshape(t * PICKS)
        gate2 = jnp.repeat(gate_t.T, 2, axis=1)
        w2 = _peer_act(off_flat, hn, tab_u, gate2)
        return _peer_out(off_flat, w2, x2, tab_v).reshape(batch, seq, D_MODEL)

    return (encode(x_prompt), encode(x_sample))
```

```python
import functools
import math

import numpy as np
import jax
import jax.numpy as jnp
from jax import lax
from jax.experimental import pallas as pl
from jax.experimental.pallas import tpu as pltpu

F32 = jnp.float32
BF16 = jnp.bfloat16

D_MODEL = 1024
HEAD_DIM = 64
N_HEADS_A = 8
N_HEADS_B = 4
SEG = 512
IN_WIDTH = 6 * SEG
LANES = 128
DILATIONS = (1, 4, 16)
BAND_HALF = 64
NUM_BUCKETS = 32
MAX_DISTANCE = 1024
N_KEYS = 128
PEER_HEADS = 8
PEER_TOPK = 16
PICKS = PEER_HEADS * PEER_TOPK
NORM_EPS = 1e-6
NEG_INF = -1e30
ATTN_SCALE = HEAD_DIM ** -0.5
LAMBDA_INIT = 0.8 - 0.6 * math.exp(-0.3 * 0)

VMEM_LIMIT = 48 * 1024 * 1024


def _dot_nt(a, b):
    return lax.dot_general(a, b, (((1,), (1,)), ((), ())), preferred_element_type=F32)


def _in_proj_kernel(x_ref, g_ref, w_ref, gsum_ref, gains_ref, o_ref):
    x = x_ref[...]
    ms = jnp.mean(x * x, axis=-1, keepdims=True)
    xn = ((x * lax.rsqrt(ms + NORM_EPS)) * g_ref[...]).astype(BF16)
    gi = 0
    for seg in range(6):
        acc = jnp.dot(xn, w_ref[:, seg * SEG:(seg + 1) * SEG], preferred_element_type=F32)
        if seg in (0, 1, 3, 4):
            sq = acc * acc
            hi = sq.astype(BF16)
            lo = (sq - hi.astype(F32)).astype(BF16)
            gs = (jnp.dot(hi, gsum_ref[...], preferred_element_type=F32)
                  + jnp.dot(lo, gsum_ref[...], preferred_element_type=F32))
            acc = (acc * lax.rsqrt(gs * (1.0 / HEAD_DIM) + NORM_EPS)) * gains_ref[gi]
            gi += 1
        o_ref[:, seg * SEG:(seg + 1) * SEG] = acc.astype(BF16)


def _in_proj(x, g, w_bf, gsum, gains, tm=512):
    t = x.shape[0]
    return pl.pallas_call(
        _in_proj_kernel,
        out_shape=jax.ShapeDtypeStruct((t, IN_WIDTH), BF16),
        grid=(t // tm,),
        in_specs=[
            pl.BlockSpec((tm, D_MODEL), lambda i: (i, 0)),
            pl.BlockSpec((1, D_MODEL), lambda i: (0, 0)),
            pl.BlockSpec((D_MODEL, IN_WIDTH), lambda i: (0, 0)),
            pl.BlockSpec((SEG, SEG), lambda i: (0, 0)),
            pl.BlockSpec((4, 1, SEG), lambda i: (0, 0, 0)),
        ],
        out_specs=pl.BlockSpec((tm, IN_WIDTH), lambda i: (i, 0)),
        compiler_params=pltpu.CompilerParams(
            dimension_semantics=("parallel",), vmem_limit_bytes=VMEM_LIMIT),
        name="in_proj",
    )(x, g, w_bf, gsum, gains)


_QT = 128
_KT = _QT + 2 * BAND_HALF
_PAD = BAND_HALF * max(DILATIONS)


def _dilated_kernel(q_ref, k_ref, v_ref, bias_ref, o_ref,
                    qf, kf, vf, qc, kc, vc, ob, lb, *, seq):
    s = seq
    lane = lax.broadcasted_iota(jnp.int32, (1, LANES), 1)
    head0 = lane < HEAD_DIM

    qf[...] = q_ref[...].astype(F32)
    zpad = jnp.zeros((_PAD, LANES), F32)
    kf[0:_PAD, :] = zpad
    kf[_PAD + s:_PAD + s + _PAD, :] = zpad
    vf[0:_PAD, :] = zpad
    vf[_PAD + s:_PAD + s + _PAD, :] = zpad
    kf[_PAD:_PAD + s, :] = k_ref[...].astype(F32)
    vf[_PAD:_PAD + s, :] = v_ref[...].astype(F32)

    for br, d in enumerate(DILATIONS):
        n = s // d
        nt = n // _QT
        def class_body(c, carry, br=br, d=d, n=n, nt=nt):
            if d == 1:
                qc[0:n, :] = q_ref[...]
                kc[0:n + 2 * BAND_HALF, :] = kf[_PAD - BAND_HALF:_PAD + s + BAND_HALF, :].astype(BF16)
                vc[0:n + 2 * BAND_HALF, :] = vf[_PAD - BAND_HALF:_PAD + s + BAND_HALF, :].astype(BF16)
            else:
                k0 = _PAD + c - BAND_HALF * d
                qc[0:n, :] = qf[pl.ds(c, n, stride=d), :].astype(BF16)
                kc[0:n + 2 * BAND_HALF, :] = kf[pl.ds(k0, n + 2 * BAND_HALF, stride=d), :].astype(BF16)
                vc[0:n + 2 * BAND_HALF, :] = vf[pl.ds(k0, n + 2 * BAND_HALF, stride=d), :].astype(BF16)

            def tile(t, br=br, d=d, c=c, n=n):
                q0 = t * _QT if isinstance(t, int) else pl.multiple_of(t * _QT, _QT)
                qt = qc[pl.ds(q0, _QT), :]
                kt = kc[pl.ds(q0, _KT), :]
                vt = vc[pl.ds(q0, _KT), :]
                kpos = lax.broadcasted_iota(jnp.int32, (1, _KT), 1) + (q0 - BAND_HALF)
                valid = (kpos >= 0) & (kpos < n)
                outs, lses = [], []
                for h in range(2):
                    qh = jnp.where(head0 if h == 0 else jnp.logical_not(head0), qt, jnp.zeros_like(qt))
                    sc = _dot_nt(qh, kt) + bias_ref[br, h]
                    sc = jnp.where(valid, sc, NEG_INF)
                    m = jnp.max(sc, axis=-1, keepdims=True)
                    p = jnp.exp(sc - m)
                    l = jnp.sum(p, axis=-1, keepdims=True)
                    o = jnp.dot(p.astype(BF16), vt, preferred_element_type=F32)
                    outs.append(o * (1.0 / l))
                    lses.append(m + jnp.log(l))
                o = jnp.where(head0, outs[0], outs[1])
                lse = jnp.where(head0, lses[0], lses[1])
                if d == 1:
                    ob[br, pl.ds(q0, _QT), :] = o
                    lb[br, pl.ds(q0, _QT), :] = lse
                else:
                    ob[br, pl.ds(c + q0 * d, _QT, stride=d), :] = o
                    lb[br, pl.ds(c + q0 * d, _QT, stride=d), :] = lse

            per_iter = 4 if nt % 4 == 0 else (2 if nt % 2 == 0 else 1)

            def tile_group(tg, carry, tile=tile, per_iter=per_iter):
                for u in range(per_iter):
                    tile(per_iter * tg + u)
                return carry

            if nt == per_iter:
                for u in range(nt):
                    tile(u)
            else:
                lax.fori_loop(0, nt // per_iter, tile_group, 0)
            return carry

        for c in range(d):
            class_body(c, 0)

    rows = 512
    for r in range(s // rows):
        sl = slice(r * rows, (r + 1) * rows)
        l0, l1, l2 = lb[0, sl, :], lb[1, sl, :], lb[2, sl, :]
        mx = jnp.maximum(jnp.maximum(l0, l1), l2)
        e0, e1, e2 = jnp.exp(l0 - mx), jnp.exp(l1 - mx), jnp.exp(l2 - mx)
        inv = 1.0 / (e0 + e1 + e2)
        out = (e0 * inv) * ob[0, sl, :] + (e1 * inv) * ob[1, sl, :] + (e2 * inv) * ob[2, sl, :]
        o_ref[sl, :] = out.astype(BF16)


def _dilated_attention(proj3, bias_tiles, batch, seq):
    nblk = SEG // LANES
    return pl.pallas_call(
        functools.partial(_dilated_kernel, seq=seq),
        out_shape=jax.ShapeDtypeStruct((batch, seq, SEG), BF16),
        grid=(batch, nblk),
        in_specs=[
            pl.BlockSpec((None, seq, LANES), lambda b, h: (b,0, h)),
            pl.BlockSpec((None, seq, LANES), lambda b, h: (b,0, nblk + h)),
            pl.BlockSpec((None, seq, LANES), lambda b, h: (b,0, 2 * nblk + h)),
            pl.BlockSpec((3, 2, _QT, _KT), lambda b, h: (0, h, 0, 0)),
        ],
        out_specs=pl.BlockSpec((None, seq, LANES), lambda b, h: (b, 0, h)),
        scratch_shapes=[
            pltpu.VMEM((seq, LANES), F32),
            pltpu.VMEM((seq + 2 * _PAD, LANES), F32),
            pltpu.VMEM((seq + 2 * _PAD, LANES), F32),
            pltpu.VMEM((seq, LANES), BF16),
            pltpu.VMEM((seq + 2 * BAND_HALF, LANES), BF16),
            pltpu.VMEM((seq + 2 * BAND_HALF, LANES), BF16),
            pltpu.VMEM((3, seq, LANES), F32),
            pltpu.VMEM((3, seq, LANES), F32),
        ],
        compiler_params=pltpu.CompilerParams(
            dimension_semantics=("parallel", "parallel"), vmem_limit_bytes=VMEM_LIMIT),
        name="dilated_attn",
    )(proj3, proj3, proj3, bias_tiles)


_DQ = 256
_DB = 4
_DCH = 8


def _diff_kernel(lam_ref, q_ref, k_ref, v_ref, bias_ref, g_ref, o_ref,
                 s1_sc, s2_sc, m1_sc, m2_sc, l1_sc, l2_sc, a1_sc, a2_sc, *, nc):
    i = pl.program_id(2)
    lane = lax.broadcasted_iota(jnp.int32, (1, LANES), 1)
    first = lane < HEAD_DIM
    q = q_ref[...]
    q1 = jnp.where(first, q, jnp.zeros_like(q))
    q2 = jnp.where(first, jnp.zeros_like(q), q)

    m1_sc[...] = jnp.full((_DQ, LANES), -jnp.inf, F32)
    m2_sc[...] = jnp.full((_DQ, LANES), -jnp.inf, F32)
    zero = jnp.zeros((_DQ, LANES), F32)
    l1_sc[...] = zero
    l2_sc[...] = zero
    a1_sc[...] = zero
    a2_sc[...] = zero

    def lane_fold(x, op):
        parts = [x[:, c * LANES:(c + 1) * LANES] for c in range(x.shape[1] // LANES)]
        while len(parts) > 1:
            parts = [op(parts[a], parts[a + 1]) for a in range(0, len(parts), 2)]
        return parts[0]

    width = _DCH * _DQ

    def score_body(jj, carry):
        j = _DCH * jj
        k0 = pl.multiple_of(j * _DQ, width)
        kj = k_ref[pl.ds(k0, width), :]
        b = jnp.concatenate([bias_ref[jnp.clip(j + u - i, -_DB, _DB) + _DB] for u in range(_DCH)], axis=1)
        s1 = _dot_nt(q1, kj) + b
        s2 = _dot_nt(q2, kj) + b
        s1_sc[jj] = s1
        s2_sc[jj] = s2
        m1_sc[...] = jnp.maximum(m1_sc[...], lane_fold(s1, jnp.maximum))
        m2_sc[...] = jnp.maximum(m2_sc[...], lane_fold(s2, jnp.maximum))
        return carry

    lax.fori_loop(0, nc // _DCH, score_body, 0)
    m1 = jnp.max(m1_sc[...], axis=-1, keepdims=True)
    m2 = jnp.max(m2_sc[...], axis=-1, keepdims=True)

    def pv_body(jj, carry):
        k0 = pl.multiple_of(jj * width, width)
        vj = v_ref[pl.ds(k0, width), :]
        p1 = jnp.exp(s1_sc[jj] - m1)
        p2 = jnp.exp(s2_sc[jj] - m2)
        l1_sc[...] += lane_fold(p1, jnp.add)
        l2_sc[...] += lane_fold(p2, jnp.add)
        a1_sc[...] += jnp.dot(p1.astype(BF16), vj, preferred_element_type=F32)
        a2_sc[...] += jnp.dot(p2.astype(BF16), vj, preferred_element_type=F32)
        return carry

    lax.fori_loop(0, nc // _DCH, pv_body, 0)
    l1 = jnp.sum(l1_sc[...], axis=-1, keepdims=True)
    l2 = jnp.sum(l2_sc[...], axis=-1, keepdims=True)
    lam = lam_ref[0]
    o = a1_sc[...] * (1.0 / l1) - (lam * a2_sc[...]) * (1.0 / l2)
    ms = jnp.mean(o * o, axis=-1, keepdims=True)
    o = ((o * lax.rsqrt(ms + NORM_EPS)) * g_ref[...]) * (1.0 - LAMBDA_INIT)
    o_ref[...] = o.astype(BF16)


def _diff_attention(lam, proj3, bias_tiles, gain, batch, seq):
    nblk = SEG // LANES
    nc = seq // _DQ
    base = 3 * nblk
    return pl.pallas_call(
        functools.partial(_diff_kernel, nc=nc),
        out_shape=jax.ShapeDtypeStruct((batch, seq, SEG), BF16),
        grid=(batch, N_HEADS_B, nc),
        in_specs=[
            pl.BlockSpec(memory_space=pltpu.SMEM),
            pl.BlockSpec((None, _DQ, LANES), lambda b, h, i: (b,i, base + h)),
            pl.BlockSpec((None, seq, LANES), lambda b, h, i: (b,0, base + nblk + h)),
            pl.BlockSpec((None, seq, LANES), lambda b, h, i: (b,0, base + 2 * nblk + h)),
            pl.BlockSpec((None, 2 * _DB + 1, _DQ, _DQ), lambda b, h, i: (h, 0, 0, 0)),
            pl.BlockSpec((1, LANES), lambda b, h, i: (0, 0)),
        ],
        out_specs=pl.BlockSpec((None, _DQ, LANES), lambda b, h, i: (b, i, h)),
        scratch_shapes=[
            pltpu.VMEM((nc // _DCH, _DQ, _DCH * _DQ), F32),
            pltpu.VMEM((nc // _DCH, _DQ, _DCH * _DQ), F32),
            pltpu.VMEM((_DQ, LANES), F32),
            pltpu.VMEM((_DQ, LANES), F32),
            pltpu.VMEM((_DQ, LANES), F32),
            pltpu.VMEM((_DQ, LANES), F32),
            pltpu.VMEM((_DQ, LANES), F32),
            pltpu.VMEM((_DQ, LANES), F32),
        ],
        compiler_params=pltpu.CompilerParams(
            dimension_semantics=("parallel", "parallel", "arbitrary"), vmem_limit_bytes=VMEM_LIMIT),
        name="diff_attn",
    )(lam, proj3, proj3, proj3, bias_tiles, gain)


def _out_proj_kernel(x_ref, oa_ref, ob_ref, wo_ref, g_ref, wq_ref, x2_ref, hn_ref, qp_ref):
    x2 = (x_ref[...]
          + jnp.dot(oa_ref[...], wo_ref[0:SEG, :], preferred_element_type=F32)
          + jnp.dot(ob_ref[...], wo_ref[SEG:2 * SEG, :], preferred_element_type=F32))
    x2_ref[...] = x2
    ms = jnp.mean(x2 * x2, axis=-1, keepdims=True)
    hn = (x2 * lax.rsqrt(ms + NORM_EPS)) * g_ref[...]
    hn_ref[...] = hn
    qp_ref[...] = jnp.dot(hn.astype(BF16), wq_ref[...], preferred_element_type=F32).astype(BF16)


def _out_proj(x, oa, ob, wo_bf, g, wq_bf, tm=512):
    t = x.shape[0]
    qw = wq_bf.shape[1]
    return pl.pallas_call(
        _out_proj_kernel,
        out_shape=(jax.ShapeDtypeStruct((t, D_MODEL), F32),
                   jax.ShapeDtypeStruct((t, D_MODEL), F32),
                   jax.ShapeDtypeStruct((t, qw), BF16)),
        grid=(t // tm,),
        in_specs=[
            pl.BlockSpec((tm, D_MODEL), lambda i: (i, 0)),
            pl.BlockSpec((tm, SEG), lambda i: (i, 0)),
            pl.BlockSpec((tm, SEG), lambda i: (i, 0)),
            pl.BlockSpec((2 * SEG, D_MODEL), lambda i: (0, 0)),
            pl.BlockSpec((1, D_MODEL), lambda i: (0, 0)),
            pl.BlockSpec((D_MODEL, qw), lambda i: (0, 0)),
        ],
        out_specs=(pl.BlockSpec((tm, D_MODEL), lambda i: (i, 0)),
                   pl.BlockSpec((tm, D_MODEL), lambda i: (i, 0)),
                   pl.BlockSpec((tm, qw), lambda i: (i, 0))),
        compiler_params=pltpu.CompilerParams(
            dimension_semantics=("parallel",), vmem_limit_bytes=VMEM_LIMIT),
        name="out_proj",
    )(x, oa, ob, wo_bf, g, wq_bf)


_CAND_ROWS = 16 + 7 * 8 + 8


def _cand_layout():
    flat = np.full((_CAND_ROWS,), -1, np.float32)
    flat[0:16] = np.arange(16)
    for a in range(1, 8):
        nb = PEER_TOPK // (a + 1)
        flat[16 + (a - 1) * 8:16 + (a - 1) * 8 + nb] = a * PEER_TOPK + np.arange(nb)
    flat[72:80] = np.arange(8, 16) * PEER_TOPK
    return flat


def _topk_kernel(qp_ref, sk_ref, cflat_ref, eidx_ref, gate_ref):
    tm = qp_ref.shape[0]
    kiota = lax.broadcasted_iota(jnp.int32, (N_KEYS, tm), 0).astype(F32)
    tv, ti = [], []
    for z in range(2):
        sc = _dot_nt(sk_ref[z], qp_ref[:, z * N_KEYS:(z + 1) * N_KEYS])
        vals, idxs = [], []
        for _ in range(PEER_TOPK):
            m = jnp.max(sc, axis=0, keepdims=True)
            idx = jnp.min(jnp.where(sc == m, kiota, float(N_KEYS)), axis=0, keepdims=True)
            vals.append(m)
            idxs.append(idx)
            sc = jnp.where(kiota == idx, -jnp.inf, sc)
        tv.append(vals)
        ti.append(idxs)

    v1_16 = jnp.concatenate(tv[1], axis=0)
    i1_16 = jnp.concatenate(ti[1], axis=0)
    v1_8, i1_8 = v1_16[0:8], i1_16[0:8]
    cand = [tv[0][0] + v1_16]
    cexp = [ti[0][0] * N_KEYS + i1_16]
    for a in range(1, 8):
        cand.append(tv[0][a] + v1_8)
        cexp.append(ti[0][a] * N_KEYS + i1_8)
    cand.append(jnp.concatenate(tv[0][8:16], axis=0) + tv[1][0])
    cexp.append(jnp.concatenate(ti[0][8:16], axis=0) * N_KEYS + ti[1][0])
    cand = jnp.concatenate(cand, axis=0)
    cexp = jnp.concatenate(cexp, axis=0)
    cflat = cflat_ref[...]
    real = cflat >= 0.0
    cand = jnp.where(real, cand, -jnp.inf)
    big = float(PEER_TOPK * PEER_TOPK)

    best, eids = [], []
    for _ in range(PEER_TOPK):
        m = jnp.max(cand, axis=0, keepdims=True)
        pos = jnp.min(jnp.where((cand == m) & real, cflat, big), axis=0, keepdims=True)
        sel = cflat == pos
        eids.append(jnp.max(jnp.where(sel, cexp, -1.0), axis=0, keepdims=True))
        best.append(m)
        cand = jnp.where(sel, -jnp.inf, cand)
    best = jnp.concatenate(best, axis=0)
    ex = jnp.exp(best - best[0:1])
    gate_ref[...] = ex * (1.0 / jnp.sum(ex, axis=0, keepdims=True))
    eidx_ref[...] = jnp.concatenate(eids, axis=0).astype(jnp.int32)


def _topk(qp, sk_bf, cflat_b, tm=512):
    t = qp.shape[0]
    return pl.pallas_call(
        _topk_kernel,
        out_shape=(jax.ShapeDtypeStruct((PICKS, t), jnp.int32),
                   jax.ShapeDtypeStruct((PICKS, t), F32)),
        grid=(t // tm, PEER_HEADS),
        in_specs=[
            pl.BlockSpec((tm, 2 * N_KEYS), lambda i, p: (i, p)),
            pl.BlockSpec((2, N_KEYS, N_KEYS), lambda i, p: (0, 0, 0)),
            pl.BlockSpec((_CAND_ROWS, tm), lambda i, p: (0, 0)),
        ],
        out_specs=(pl.BlockSpec((PEER_TOPK, tm), lambda i, p: (p, i)),
                   pl.BlockSpec((PEER_TOPK, tm), lambda i, p: (p, i))),
        compiler_params=pltpu.CompilerParams(
            dimension_semantics=("parallel", "parallel"), vmem_limit_bytes=VMEM_LIMIT),
        name="peer_topk",
    )(qp, sk_bf, cflat_b)


_HALF = D_MODEL // 2
_HROWS = _HALF // LANES
_TT = 128
_GRP = 8
_NGRP = _TT // _GRP
_OFFS = 8


def _gather_group(off_ref, tab_ref, stg, g):
    for j in range(_GRP):
        for c in range(PICKS // _OFFS):
            row = off_ref.at[pl.ds(pl.multiple_of((g * _GRP + j) * PICKS + c * _OFFS, _OFFS), _OFFS)]
            for kk in range(0, _OFFS, 2):
                k = c * _OFFS + kk
                pair = [tab_ref[pl.ds(pl.multiple_of(row[kk + u], _HROWS), _HROWS), :] for u in range(2)]
                stg[j][k * _HROWS:(k + 2) * _HROWS, :] = jnp.concatenate(pair, axis=0)


def _staged_matrix(stg):
    cols = [pltpu.bitcast(stg[pl.ds(s, PICKS, stride=_HROWS), :], BF16) for s in range(_HROWS)]
    return jnp.concatenate(cols, axis=1)


def _split_bf16(x):
    hi = x.astype(BF16)
    lo = (x - hi.astype(F32)).astype(BF16)
    return hi, lo


def _swap_pairs(x):
    even = (lax.broadcasted_iota(jnp.int32, (1, LANES), 1) & 1) == 0
    out = []
    for c in range(x.shape[1] // LANES):
        blk = x[:, c * LANES:(c + 1) * LANES]
        out.append(jnp.where(even, pltpu.roll(blk, LANES - 1, 1), pltpu.roll(blk, 1, 1)))
    return jnp.concatenate(out, axis=1)


def _staged_pipeline(head_ref, idx_ref, tab_ref, stg, dot_group):
    @pl.when(pl.program_id(0) == 0)
    def _():
        _gather_group(head_ref, tab_ref, stg, 0)

    def body(g, carry):
        dot_group(g)
        _gather_group(idx_ref, tab_ref, stg, g)
        return carry

    lax.fori_loop(0, _NGRP, body, 0)


def _peer_act_kernel(head_ref, idx_ref, x_ref, tab_ref, gate_ref, w_ref, *stg):
    even = (lax.broadcasted_iota(jnp.int32, (1, 2 * PICKS), 1) & 1) == 0

    def dot_group(g):
        r0 = pl.multiple_of(g * _GRP, _GRP)
        hi, lo = _split_bf16(x_ref[pl.ds(r0, _GRP), :])
        lhs = jnp.concatenate([hi[:, :_HALF], hi[:, _HALF:], lo[:, :_HALF], lo[:, _HALF:]], axis=0)
        tok = lax.broadcasted_iota(jnp.int32, lhs.shape, 0) & (_GRP - 1)
        r = jnp.zeros((4 * _GRP, 2 * PICKS), F32)
        for j in range(_GRP):
            r = r + _dot_nt(jnp.where(tok == j, lhs, jnp.zeros_like(lhs)), _staged_matrix(stg[j]))
        first = r[0:_GRP] + r[2 * _GRP:3 * _GRP]
        second = r[_GRP:2 * _GRP] + r[3 * _GRP:4 * _GRP]
        w_ref[pl.ds(r0, _GRP), :] = jnp.where(even, first, second)

    _staged_pipeline(head_ref, idx_ref, tab_ref, stg, dot_group)

    part = w_ref[...]
    h = part + _swap_pairs(part)
    act = 0.5 * h * (1.0 + lax.erf(h * (2.0 ** -0.5)))
    w_ref[...] = gate_ref[...] * act


def _peer_out_kernel(head_ref, idx_ref, w_ref, x2_ref, tab_ref, y_ref, *stg):
    even = (lax.broadcasted_iota(jnp.int32, (1, 2 * PICKS), 1) & 1) == 0

    def dot_group(g):
        r0 = pl.multiple_of(g * _GRP, _GRP)
        hi, lo = _split_bf16(w_ref[pl.ds(r0, _GRP), :])
        zero = jnp.zeros_like(hi)
        lhs = jnp.concatenate([jnp.where(even, hi, zero), jnp.where(even, zero, hi),
                               jnp.where(even, lo, zero), jnp.where(even, zero, lo)], axis=0)
        tok = lax.broadcasted_iota(jnp.int32, lhs.shape, 0) & (_GRP - 1)
        r = jnp.zeros((4 * _GRP, _HALF), F32)
        for j in range(_GRP):
            r = r + jnp.dot(jnp.where(tok == j, lhs, jnp.zeros_like(lhs)), _staged_matrix(stg[j]),
                            preferred_element_type=F32)
        upd = jnp.concatenate([r[0:_GRP] + r[2 * _GRP:3 * _GRP], r[_GRP:2 * _GRP] + r[3 * _GRP:4 * _GRP]], axis=1)
        y_ref[pl.ds(r0, _GRP), :] = x2_ref[pl.ds(r0, _GRP), :] + upd

    _staged_pipeline(head_ref, idx_ref, tab_ref, stg, dot_group)


def _table_spec(rows):
    return pl.BlockSpec((rows, LANES), lambda i: (0, 0), pipeline_mode=pl.Buffered(1))


def _staging():
    return [pltpu.VMEM((PICKS * _HROWS, LANES), jnp.uint32) for _ in range(_GRP)]


def _offset_specs():
    return [pl.BlockSpec((_GRP * PICKS,), lambda i: (0,), memory_space=pltpu.SMEM),
            pl.BlockSpec((_TT * PICKS,), lambda i: (i,), memory_space=pltpu.SMEM)]


def _shift_offsets(off_flat):
    head = off_flat[:_GRP * PICKS]
    shifted = jnp.concatenate([off_flat[_GRP * PICKS:], jnp.zeros((_GRP * PICKS,), off_flat.dtype)])
    return head, shifted


def _peer_act(off_head, off_shift, hn, tab, gate2):
    t = hn.shape[0]
    return pl.pallas_call(
        _peer_act_kernel,
        out_shape=jax.ShapeDtypeStruct((t, 2 * PICKS), F32),
        grid=(t // _TT,),
        in_specs=_offset_specs() + [
            pl.BlockSpec((_TT, D_MODEL), lambda i: (i, 0)),
            _table_spec(tab.shape[0]),
            pl.BlockSpec((_TT, 2 * PICKS), lambda i: (i, 0)),
        ],
        out_specs=pl.BlockSpec((_TT, 2 * PICKS), lambda i: (i, 0)),
        scratch_shapes=_staging(),
        compiler_params=pltpu.CompilerParams(
            dimension_semantics=("arbitrary",), vmem_limit_bytes=VMEM_LIMIT),
        name="peer_act",
    )(off_head, off_shift, hn, tab, gate2)


def _peer_out(off_head, off_shift, w2, x2, tab):
    t = x2.shape[0]
    return pl.pallas_call(
        _peer_out_kernel,
        out_shape=jax.ShapeDtypeStruct(x2.shape, F32),
        grid=(t // _TT,),
        in_specs=_offset_specs() + [
            pl.BlockSpec((_TT, 2 * PICKS), lambda i: (i, 0)),
            pl.BlockSpec((_TT, D_MODEL), lambda i: (i, 0)),
            _table_spec(tab.shape[0]),
        ],
        out_specs=pl.BlockSpec((_TT, D_MODEL), lambda i: (i, 0)),
        scratch_shapes=_staging(),
        compiler_params=pltpu.CompilerParams(
            dimension_semantics=("arbitrary",), vmem_limit_bytes=VMEM_LIMIT),
        name="peer_out",
    )(off_head, off_shift, w2, x2, tab)


def _rel_bucket(rel):
    nb = NUM_BUCKETS // 2
    max_exact = nb // 2
    n = jnp.abs(rel)
    large = max_exact + (jnp.log(jnp.maximum(n, 1).astype(F32) / max_exact)
                         / math.log(MAX_DISTANCE / max_exact) * (nb - max_exact)).astype(jnp.int32)
    large = jnp.minimum(large, nb - 1)
    return (rel > 0).astype(jnp.int32) * nb + jnp.where(n < max_exact, n, large)


def _bucket_saturation():
    nb = NUM_BUCKETS // 2
    max_exact = nb // 2
    n = np.arange(1, 8192, dtype=np.float64)
    large = max_exact + np.floor(np.log(n / max_exact) / math.log(MAX_DISTANCE / max_exact) * (nb - max_exact))
    return int(n[np.argmax(large >= nb - 1)])


def _toeplitz(v, nq, nk, k0):
    h = v.shape[0]
    width = nq + nk - 1
    w = v[:, k0 - (nq - 1):k0 + nk]
    w = jnp.concatenate([w, jnp.zeros((h, 1), v.dtype)], axis=1)
    rows = jnp.tile(w, (1, nq))[:, :nq * width].reshape(h, nq, width)
    return rows[:, :, nq - 1:nq - 1 + nk]


def _dilated_bias_tiles(bias_a):
    reach = _QT + BAND_HALF
    off = np.arange(-reach, reach + 1)
    tiles = []
    for d in DILATIONS:
        v = bias_a[_rel_bucket(jnp.asarray(off * d, jnp.int32))].astype(F32)
        v = jnp.where(jnp.asarray(np.abs(off) <= BAND_HALF)[:, None], v, NEG_INF)
        tiles.append(_toeplitz(v.T, _QT, _KT, reach - BAND_HALF))
    return jnp.stack(tiles, 0)


def _diff_bias_tiles(bias_b):
    assert (_DB - 1) * _DQ + 1 >= _bucket_saturation() + _DQ // 2
    reach = (_DB + 1) * _DQ
    rel = np.arange(-reach, reach + 1)
    v = bias_b[_rel_bucket(jnp.asarray(rel, jnp.int32))].astype(F32).T
    tiles = [_toeplitz(v, _DQ, _DQ, reach + delta * _DQ) for delta in range(-_DB, _DB + 1)]
    return jnp.stack(tiles, 1)


def _pack_table(w):
    e = w.shape[0]
    bits = lax.bitcast_convert_type(w.astype(BF16), jnp.uint16).astype(jnp.uint32)
    packed = bits[:, :_HALF] | (bits[:, _HALF:] << 16)
    return packed.reshape(e * _HROWS, LANES)


def _tile_gain(g, scale=1.0):
    return jnp.tile(g.astype(F32) * scale, SEG // HEAD_DIM).reshape(1, SEG)


def kernel(x_prompt, x_sample, rel_bias, attn_norm_g, w_in, q_norm_a, k_norm_a, q_norm_b, k_norm_b,
           lambda_q1, lambda_k1, lambda_q2, lambda_k2, diff_norm_g, w_out, ffn_norm_g,
           peer_w_q, peer_sub_keys, peer_u, peer_v):
    gsum = jnp.asarray(np.kron(np.eye(SEG // HEAD_DIM), np.ones((HEAD_DIM, HEAD_DIM))), BF16)
    gains = jnp.stack([_tile_gain(q_norm_a[0], ATTN_SCALE), _tile_gain(k_norm_a[0]),
                       _tile_gain(q_norm_b[0], ATTN_SCALE), _tile_gain(k_norm_b[0])], 0)
    attn_g = attn_norm_g[0].reshape(1, D_MODEL).astype(F32)
    ffn_g = ffn_norm_g[0].reshape(1, D_MODEL).astype(F32)
    w_in_bf, w_out_bf, w_q_bf = w_in[0].astype(BF16), w_out[0].astype(BF16), peer_w_q[0].astype(BF16)
    sk_bf = peer_sub_keys[0].astype(BF16)
    bias_a = _dilated_bias_tiles(rel_bias[:, :N_HEADS_A])
    bias_b = _diff_bias_tiles(rel_bias[:, N_HEADS_A:])
    lam = (jnp.exp(jnp.sum(lambda_q1[0].astype(F32) * lambda_k1[0].astype(F32)))
           - jnp.exp(jnp.sum(lambda_q2[0].astype(F32) * lambda_k2[0].astype(F32)))
           + LAMBDA_INIT).reshape(1)
    dgain = diff_norm_g[0].astype(F32).reshape(1, LANES)
    tab_u, tab_v = _pack_table(peer_u[0]), _pack_table(peer_v[0])
    tm = 512
    cflat_b = jnp.asarray(np.broadcast_to(_cand_layout()[:, None], (_CAND_ROWS, tm)).copy())

    def encode(x3):
        batch, seq, _ = x3.shape
        t = batch * seq
        x = x3.reshape(t, D_MODEL)
        proj3 = _in_proj(x, attn_g, w_in_bf, gsum, gains).reshape(batch, seq, IN_WIDTH)
        oa = _dilated_attention(proj3, bias_a, batch, seq).reshape(t, SEG)
        ob = _diff_attention(lam, proj3, bias_b, dgain, batch, seq).reshape(t, SEG)
        x2, hn, qp = _out_proj(x, oa, ob, w_out_bf, ffn_g, w_q_bf)
        eidx_t, gate_t = _topk(qp, sk_bf, cflat_b, tm)
        off_flat = (eidx_t.T * _HROWS).reshape(t * PICKS)
        gate2 = jnp.repeat(gate_t.T, 2, axis=1)
        off_head, off_shift = _shift_offsets(off_flat)
        w2 = _peer_act(off_head, off_shift, hn, tab_u, gate2)
        return _peer_out(off_head, off_shift, w2, x2, tab_v).reshape(batch, seq, D_MODEL)

    return (encode(x_prompt), encode(x_sample))
```

```python
import functools
import math

import numpy as np
import jax
import jax.numpy as jnp
from jax import lax
from jax.experimental import pallas as pl
from jax.experimental.pallas import tpu as pltpu

F32 = jnp.float32
BF16 = jnp.bfloat16

D_MODEL = 1024
HEAD_DIM = 64
N_HEADS_A = 8
N_HEADS_B = 4
SEG = 512
IN_WIDTH = 6 * SEG
LANES = 128
DILATIONS = (1, 4, 16)
BAND_HALF = 64
NUM_BUCKETS = 32
MAX_DISTANCE = 1024
N_KEYS = 128
PEER_HEADS = 8
PEER_TOPK = 16
PICKS = PEER_HEADS * PEER_TOPK
NORM_EPS = 1e-6
NEG_INF = -1e30
ATTN_SCALE = HEAD_DIM ** -0.5
LAMBDA_INIT = 0.8 - 0.6 * math.exp(-0.3 * 0)

VMEM_LIMIT = 48 * 1024 * 1024


def _dot_nt(a, b):
    return lax.dot_general(a, b, (((1,), (1,)), ((), ())), preferred_element_type=F32)


def _in_proj_kernel(x_ref, g_ref, w_ref, gsum_ref, gains_ref, o_ref):
    x = x_ref[...]
    ms = jnp.mean(x * x, axis=-1, keepdims=True)
    xn = ((x * lax.rsqrt(ms + NORM_EPS)) * g_ref[...]).astype(BF16)
    gi = 0
    for seg in range(6):
        acc = jnp.dot(xn, w_ref[:, seg * SEG:(seg + 1) * SEG], preferred_element_type=F32)
        if seg in (0, 1, 3, 4):
            sq = acc * acc
            hi = sq.astype(BF16)
            lo = (sq - hi.astype(F32)).astype(BF16)
            gs = (jnp.dot(hi, gsum_ref[...], preferred_element_type=F32)
                  + jnp.dot(lo, gsum_ref[...], preferred_element_type=F32))
            acc = (acc * lax.rsqrt(gs * (1.0 / HEAD_DIM) + NORM_EPS)) * gains_ref[gi]
            gi += 1
        o_ref[:, seg * SEG:(seg + 1) * SEG] = acc.astype(BF16)


def _in_proj(x, g, w_bf, gsum, gains, tm=512):
    t = x.shape[0]
    return pl.pallas_call(
        _in_proj_kernel,
        out_shape=jax.ShapeDtypeStruct((t, IN_WIDTH), BF16),
        grid=(t // tm,),
        in_specs=[
            pl.BlockSpec((tm, D_MODEL), lambda i: (i, 0)),
            pl.BlockSpec((1, D_MODEL), lambda i: (0, 0)),
            pl.BlockSpec((D_MODEL, IN_WIDTH), lambda i: (0, 0)),
            pl.BlockSpec((SEG, SEG), lambda i: (0, 0)),
            pl.BlockSpec((4, 1, SEG), lambda i: (0, 0, 0)),
        ],
        out_specs=pl.BlockSpec((tm, IN_WIDTH), lambda i: (i, 0)),
        compiler_params=pltpu.CompilerParams(
            dimension_semantics=("parallel",), vmem_limit_bytes=VMEM_LIMIT),
        name="in_proj",
    )(x, g, w_bf, gsum, gains)


_QT = 128
_KT = _QT + 2 * BAND_HALF
_PAD = BAND_HALF * max(DILATIONS)


def _dilated_kernel(q_ref, k_ref, v_ref, bias_ref, o_ref,
                    qf, kf, vf, qc, kc, vc, ob, lb, *, seq):
    s = seq
    lane = lax.broadcasted_iota(jnp.int32, (1, LANES), 1)
    head0 = lane < HEAD_DIM

    qf[...] = q_ref[...].astype(F32)
    zpad = jnp.zeros((_PAD, LANES), F32)
    kf[0:_PAD, :] = zpad
    kf[_PAD + s:_PAD + s + _PAD, :] = zpad
    vf[0:_PAD, :] = zpad
    vf[_PAD + s:_PAD + s + _PAD, :] = zpad
    kf[_PAD:_PAD + s, :] = k_ref[...].astype(F32)
    vf[_PAD:_PAD + s, :] = v_ref[...].astype(F32)

    for br, d in enumerate(DILATIONS):
        n = s // d
        nt = n // _QT
        def class_body(c, carry, br=br, d=d, n=n, nt=nt):
            if d == 1:
                qc[0:n, :] = q_ref[...]
                kc[0:n + 2 * BAND_HALF, :] = kf[_PAD - BAND_HALF:_PAD + s + BAND_HALF, :].astype(BF16)
                vc[0:n + 2 * BAND_HALF, :] = vf[_PAD - BAND_HALF:_PAD + s + BAND_HALF, :].astype(BF16)
            else:
                k0 = _PAD + c - BAND_HALF * d
                qc[0:n, :] = qf[pl.ds(c, n, stride=d), :].astype(BF16)
                kc[0:n + 2 * BAND_HALF, :] = kf[pl.ds(k0, n + 2 * BAND_HALF, stride=d), :].astype(BF16)
                vc[0:n + 2 * BAND_HALF, :] = vf[pl.ds(k0, n + 2 * BAND_HALF, stride=d), :].astype(BF16)

            def tile(t, br=br, d=d, c=c, n=n):
                q0 = t * _QT if isinstance(t, int) else pl.multiple_of(t * _QT, _QT)
                qt = qc[pl.ds(q0, _QT), :]
                kt = kc[pl.ds(q0, _KT), :]
                vt = vc[pl.ds(q0, _KT), :]
                kpos = lax.broadcasted_iota(jnp.int32, (1, _KT), 1) + (q0 - BAND_HALF)
                valid = (kpos >= 0) & (kpos < n)
                outs, lses = [], []
                for h in range(2):
                    qh = jnp.where(head0 if h == 0 else jnp.logical_not(head0), qt, jnp.zeros_like(qt))
                    sc = _dot_nt(qh, kt) + bias_ref[br, h]
                    sc = jnp.where(valid, sc, NEG_INF)
                    m = jnp.max(sc, axis=-1, keepdims=True)
                    p = jnp.exp(sc - m)
                    l = jnp.sum(p, axis=-1, keepdims=True)
                    o = jnp.dot(p.astype(BF16), vt, preferred_element_type=F32)
                    outs.append(o * (1.0 / l))
                    lses.append(m + jnp.log(l))
                o = jnp.where(head0, outs[0], outs[1])
                lse = jnp.where(head0, lses[0], lses[1])
                if d == 1:
                    ob[br, pl.ds(q0, _QT), :] = o
                    lb[br, pl.ds(q0, _QT), :] = lse
                else:
                    ob[br, pl.ds(c + q0 * d, _QT, stride=d), :] = o
                    lb[br, pl.ds(c + q0 * d, _QT, stride=d), :] = lse

            per_iter = next(u for u in (8, 4, 2, 1) if nt % u == 0)

            def tile_group(tg, carry, tile=tile, per_iter=per_iter):
                for u in range(per_iter):
                    tile(per_iter * tg + u)
                return carry

            if nt == per_iter:
                for u in range(nt):
                    tile(u)
            else:
                lax.fori_loop(0, nt // per_iter, tile_group, 0)
            return carry

        for c in range(d):
            class_body(c, 0)

    rows = 512
    for r in range(s // rows):
        sl = slice(r * rows, (r + 1) * rows)
        l0, l1, l2 = lb[0, sl, :], lb[1, sl, :], lb[2, sl, :]
        mx = jnp.maximum(jnp.maximum(l0, l1), l2)
        e0, e1, e2 = jnp.exp(l0 - mx), jnp.exp(l1 - mx), jnp.exp(l2 - mx)
        inv = 1.0 / (e0 + e1 + e2)
        out = (e0 * inv) * ob[0, sl, :] + (e1 * inv) * ob[1, sl, :] + (e2 * inv) * ob[2, sl, :]
        o_ref[sl, :] = out.astype(BF16)


def _dilated_attention(proj3, bias_tiles, batch, seq):
    nblk = SEG // LANES
    return pl.pallas_call(
        functools.partial(_dilated_kernel, seq=seq),
        out_shape=jax.ShapeDtypeStruct((batch, seq, SEG), BF16),
        grid=(batch, nblk),
        in_specs=[
            pl.BlockSpec((None, seq, LANES), lambda b, h: (b,0, h)),
            pl.BlockSpec((None, seq, LANES), lambda b, h: (b,0, nblk + h)),
            pl.BlockSpec((None, seq, LANES), lambda b, h: (b,0, 2 * nblk + h)),
            pl.BlockSpec((3, 2, _QT, _KT), lambda b, h: (0, h, 0, 0)),
        ],
        out_specs=pl.BlockSpec((None, seq, LANES), lambda b, h: (b, 0, h)),
        scratch_shapes=[
            pltpu.VMEM((seq, LANES), F32),
            pltpu.VMEM((seq + 2 * _PAD, LANES), F32),
            pltpu.VMEM((seq + 2 * _PAD, LANES), F32),
            pltpu.VMEM((seq, LANES), BF16),
            pltpu.VMEM((seq + 2 * BAND_HALF, LANES), BF16),
            pltpu.VMEM((seq + 2 * BAND_HALF, LANES), BF16),
            pltpu.VMEM((3, seq, LANES), F32),
            pltpu.VMEM((3, seq, LANES), F32),
        ],
        compiler_params=pltpu.CompilerParams(
            dimension_semantics=("parallel", "parallel"), vmem_limit_bytes=VMEM_LIMIT),
        name="dilated_attn",
    )(proj3, proj3, proj3, bias_tiles)


_DQ = 256
_DB = 4
_DCH = 8


def _diff_kernel(lam_ref, q_ref, k_ref, v_ref, bias_ref, g_ref, o_ref,
                 s1_sc, s2_sc, m1_sc, m2_sc, l1_sc, l2_sc, a1_sc, a2_sc, *, nc):
    i = pl.program_id(2)
    lane = lax.broadcasted_iota(jnp.int32, (1, LANES), 1)
    first = lane < HEAD_DIM
    q = q_ref[...]
    q1 = jnp.where(first, q, jnp.zeros_like(q))
    q2 = jnp.where(first, jnp.zeros_like(q), q)

    m1_sc[...] = jnp.full((_DQ, LANES), -jnp.inf, F32)
    m2_sc[...] = jnp.full((_DQ, LANES), -jnp.inf, F32)
    zero = jnp.zeros((_DQ, LANES), F32)
    l1_sc[...] = zero
    l2_sc[...] = zero
    a1_sc[...] = zero
    a2_sc[...] = zero

    def lane_fold(x, op):
        parts = [x[:, c * LANES:(c + 1) * LANES] for c in range(x.shape[1] // LANES)]
        while len(parts) > 1:
            parts = [op(parts[a], parts[a + 1]) for a in range(0, len(parts), 2)]
        return parts[0]

    width = _DCH * _DQ

    def score_body(jj, carry):
        j = _DCH * jj
        k0 = pl.multiple_of(j * _DQ, width)
        kj = k_ref[pl.ds(k0, width), :]
        b = jnp.concatenate([bias_ref[jnp.clip(j + u - i, -_DB, _DB) + _DB] for u in range(_DCH)], axis=1)
        s1 = _dot_nt(q1, kj) + b
        s2 = _dot_nt(q2, kj) + b
        s1_sc[jj] = s1
        s2_sc[jj] = s2
        m1_sc[...] = jnp.maximum(m1_sc[...], lane_fold(s1, jnp.maximum))
        m2_sc[...] = jnp.maximum(m2_sc[...], lane_fold(s2, jnp.maximum))
        return carry

    lax.fori_loop(0, nc // _DCH, score_body, 0)
    m1 = jnp.max(m1_sc[...], axis=-1, keepdims=True)
    m2 = jnp.max(m2_sc[...], axis=-1, keepdims=True)

    def pv_body(jj, carry):
        k0 = pl.multiple_of(jj * width, width)
        vj = v_ref[pl.ds(k0, width), :]
        p1 = jnp.exp(s1_sc[jj] - m1)
        p2 = jnp.exp(s2_sc[jj] - m2)
        l1_sc[...] += lane_fold(p1, jnp.add)
        l2_sc[...] += lane_fold(p2, jnp.add)
        a1_sc[...] += jnp.dot(p1.astype(BF16), vj, preferred_element_type=F32)
        a2_sc[...] += jnp.dot(p2.astype(BF16), vj, preferred_element_type=F32)
        return carry

    lax.fori_loop(0, nc // _DCH, pv_body, 0)
    l1 = jnp.sum(l1_sc[...], axis=-1, keepdims=True)
    l2 = jnp.sum(l2_sc[...], axis=-1, keepdims=True)
    lam = lam_ref[0]
    o = a1_sc[...] * (1.0 / l1) - (lam * a2_sc[...]) * (1.0 / l2)
    ms = jnp.mean(o * o, axis=-1, keepdims=True)
    o = ((o * lax.rsqrt(ms + NORM_EPS)) * g_ref[...]) * (1.0 - LAMBDA_INIT)
    o_ref[...] = o.astype(BF16)


def _diff_attention(lam, proj3, bias_tiles, gain, batch, seq):
    nblk = SEG // LANES
    nc = seq // _DQ
    base = 3 * nblk
    return pl.pallas_call(
        functools.partial(_diff_kernel, nc=nc),
        out_shape=jax.ShapeDtypeStruct((batch, seq, SEG), BF16),
        grid=(batch, N_HEADS_B, nc),
        in_specs=[
            pl.BlockSpec(memory_space=pltpu.SMEM),
            pl.BlockSpec((None, _DQ, LANES), lambda b, h, i: (b,i, base + h)),
            pl.BlockSpec((None, seq, LANES), lambda b, h, i: (b,0, base + nblk + h)),
            pl.BlockSpec((None, seq, LANES), lambda b, h, i: (b,0, base + 2 * nblk + h)),
            pl.BlockSpec((None, 2 * _DB + 1, _DQ, _DQ), lambda b, h, i: (h, 0, 0, 0)),
            pl.BlockSpec((1, LANES), lambda b, h, i: (0, 0)),
        ],
        out_specs=pl.BlockSpec((None, _DQ, LANES), lambda b, h, i: (b, i, h)),
        scratch_shapes=[
            pltpu.VMEM((nc // _DCH, _DQ, _DCH * _DQ), F32),
            pltpu.VMEM((nc // _DCH, _DQ, _DCH * _DQ), F32),
            pltpu.VMEM((_DQ, LANES), F32),
            pltpu.VMEM((_DQ, LANES), F32),
            pltpu.VMEM((_DQ, LANES), F32),
            pltpu.VMEM((_DQ, LANES), F32),
            pltpu.VMEM((_DQ, LANES), F32),
            pltpu.VMEM((_DQ, LANES), F32),
        ],
        compiler_params=pltpu.CompilerParams(
            dimension_semantics=("parallel", "parallel", "arbitrary"), vmem_limit_bytes=VMEM_LIMIT),
        name="diff_attn",
    )(lam, proj3, proj3, proj3, bias_tiles, gain)


def _out_proj_kernel(x_ref, oa_ref, ob_ref, wo_ref, g_ref, wq_ref, x2_ref, hn_ref, qp_ref):
    x2 = (x_ref[...]
          + jnp.dot(oa_ref[...], wo_ref[0:SEG, :], preferred_element_type=F32)
          + jnp.dot(ob_ref[...], wo_ref[SEG:2 * SEG, :], preferred_element_type=F32))
    x2_ref[...] = x2
    ms = jnp.mean(x2 * x2, axis=-1, keepdims=True)
    hn = (x2 * lax.rsqrt(ms + NORM_EPS)) * g_ref[...]
    hn_ref[...] = hn
    qp_ref[...] = jnp.dot(hn.astype(BF16), wq_ref[...], preferred_element_type=F32).astype(BF16)


def _out_proj(x, oa, ob, wo_bf, g, wq_bf, tm=512):
    t = x.shape[0]
    qw = wq_bf.shape[1]
    return pl.pallas_call(
        _out_proj_kernel,
        out_shape=(jax.ShapeDtypeStruct((t, D_MODEL), F32),
                   jax.ShapeDtypeStruct((t, D_MODEL), F32),
                   jax.ShapeDtypeStruct((t, qw), BF16)),
        grid=(t // tm,),
        in_specs=[
            pl.BlockSpec((tm, D_MODEL), lambda i: (i, 0)),
            pl.BlockSpec((tm, SEG), lambda i: (i, 0)),
            pl.BlockSpec((tm, SEG), lambda i: (i, 0)),
            pl.BlockSpec((2 * SEG, D_MODEL), lambda i: (0, 0)),
            pl.BlockSpec((1, D_MODEL), lambda i: (0, 0)),
            pl.BlockSpec((D_MODEL, qw), lambda i: (0, 0)),
        ],
        out_specs=(pl.BlockSpec((tm, D_MODEL), lambda i: (i, 0)),
                   pl.BlockSpec((tm, D_MODEL), lambda i: (i, 0)),
                   pl.BlockSpec((tm, qw), lambda i: (i, 0))),
        compiler_params=pltpu.CompilerParams(
            dimension_semantics=("parallel",), vmem_limit_bytes=VMEM_LIMIT),
        name="out_proj",
    )(x, oa, ob, wo_bf, g, wq_bf)


_CAND_ROWS = 16 + 7 * 8 + 8


def _cand_layout():
    flat = np.full((_CAND_ROWS,), -1, np.float32)
    flat[0:16] = np.arange(16)
    for a in range(1, 8):
        nb = PEER_TOPK // (a + 1)
        flat[16 + (a - 1) * 8:16 + (a - 1) * 8 + nb] = a * PEER_TOPK + np.arange(nb)
    flat[72:80] = np.arange(8, 16) * PEER_TOPK
    return flat


def _topk_kernel(qp_ref, sk_ref, cflat_ref, eidx_ref, gate_ref):
    tm = qp_ref.shape[0]
    kiota = lax.broadcasted_iota(jnp.int32, (N_KEYS, tm), 0).astype(F32)
    tv, ti = [], []
    for z in range(2):
        sc = _dot_nt(sk_ref[z], qp_ref[:, z * N_KEYS:(z + 1) * N_KEYS])
        vals, idxs = [], []
        for _ in range(PEER_TOPK):
            m = jnp.max(sc, axis=0, keepdims=True)
            idx = jnp.min(jnp.where(sc == m, kiota, float(N_KEYS)), axis=0, keepdims=True)
            vals.append(m)
            idxs.append(idx)
            sc = jnp.where(kiota == idx, -jnp.inf, sc)
        tv.append(vals)
        ti.append(idxs)

    v1_16 = jnp.concatenate(tv[1], axis=0)
    i1_16 = jnp.concatenate(ti[1], axis=0)
    v1_8, i1_8 = v1_16[0:8], i1_16[0:8]
    cand = [tv[0][0] + v1_16]
    cexp = [ti[0][0] * N_KEYS + i1_16]
    for a in range(1, 8):
        cand.append(tv[0][a] + v1_8)
        cexp.append(ti[0][a] * N_KEYS + i1_8)
    cand.append(jnp.concatenate(tv[0][8:16], axis=0) + tv[1][0])
    cexp.append(jnp.concatenate(ti[0][8:16], axis=0) * N_KEYS + ti[1][0])
    cand = jnp.concatenate(cand, axis=0)
    cexp = jnp.concatenate(cexp, axis=0)
    cflat = cflat_ref[...]
    real = cflat >= 0.0
    cand = jnp.where(real, cand, -jnp.inf)
    big = float(PEER_TOPK * PEER_TOPK)

    best, eids = [], []
    for _ in range(PEER_TOPK):
        m = jnp.max(cand, axis=0, keepdims=True)
        pos = jnp.min(jnp.where((cand == m) & real, cflat, big), axis=0, keepdims=True)
        sel = cflat == pos
        eids.append(jnp.max(jnp.where(sel, cexp, -1.0), axis=0, keepdims=True))
        best.append(m)
        cand = jnp.where(sel, -jnp.inf, cand)
    best = jnp.concatenate(best, axis=0)
    ex = jnp.exp(best - best[0:1])
    gate_ref[...] = ex * (1.0 / jnp.sum(ex, axis=0, keepdims=True))
    eidx_ref[...] = jnp.concatenate(eids, axis=0).astype(jnp.int32)


def _topk(qp, sk_bf, cflat_b, tm=512):
    t = qp.shape[0]
    return pl.pallas_call(
        _topk_kernel,
        out_shape=(jax.ShapeDtypeStruct((PICKS, t), jnp.int32),
                   jax.ShapeDtypeStruct((PICKS, t), F32)),
        grid=(t // tm, PEER_HEADS),
        in_specs=[
            pl.BlockSpec((tm, 2 * N_KEYS), lambda i, p: (i, p)),
            pl.BlockSpec((2, N_KEYS, N_KEYS), lambda i, p: (0, 0, 0)),
            pl.BlockSpec((_CAND_ROWS, tm), lambda i, p: (0, 0)),
        ],
        out_specs=(pl.BlockSpec((PEER_TOPK, tm), lambda i, p: (p, i)),
                   pl.BlockSpec((PEER_TOPK, tm), lambda i, p: (p, i))),
        compiler_params=pltpu.CompilerParams(
            dimension_semantics=("parallel", "parallel"), vmem_limit_bytes=VMEM_LIMIT),
        name="peer_topk",
    )(qp, sk_bf, cflat_b)


_HALF = D_MODEL // 2
_HROWS = _HALF // LANES
_TT = 128
_GRP = 8
_NGRP = _TT // _GRP
_OFFS = 8


def _gather_group(off_ref, tab_ref, stg, g):
    for j in range(_GRP):
        for c in range(PICKS // _OFFS):
            row = off_ref.at[pl.ds(pl.multiple_of((g * _GRP + j) * PICKS + c * _OFFS, _OFFS), _OFFS)]
            for kk in range(0, _OFFS, 2):
                k = c * _OFFS + kk
                pair = [tab_ref[pl.ds(pl.multiple_of(row[kk + u], _HROWS), _HROWS), :] for u in range(2)]
                stg[j][k * _HROWS:(k + 2) * _HROWS, :] = jnp.concatenate(pair, axis=0)


def _staged_matrix(stg):
    cols = [pltpu.bitcast(stg[pl.ds(s, PICKS, stride=_HROWS), :], BF16) for s in range(_HROWS)]
    return jnp.concatenate(cols, axis=1)


def _split_bf16(x):
    hi = x.astype(BF16)
    lo = (x - hi.astype(F32)).astype(BF16)
    return hi, lo


def _swap_pairs(x):
    even = (lax.broadcasted_iota(jnp.int32, (1, LANES), 1) & 1) == 0
    out = []
    for c in range(x.shape[1] // LANES):
        blk = x[:, c * LANES:(c + 1) * LANES]
        out.append(jnp.where(even, pltpu.roll(blk, LANES - 1, 1), pltpu.roll(blk, 1, 1)))
    return jnp.concatenate(out, axis=1)


def _staged_pipeline(head_ref, idx_ref, tab_ref, stg, dot_group):
    @pl.when(pl.program_id(0) == 0)
    def _():
        _gather_group(head_ref, tab_ref, stg, 0)

    def body(g, carry):
        dot_group(g)
        _gather_group(idx_ref, tab_ref, stg, g)
        return carry

    lax.fori_loop(0, _NGRP, body, 0)


def _peer_act_kernel(head_ref, idx_ref, x_ref, tab_ref, gate_ref, w_ref, *stg):
    even = (lax.broadcasted_iota(jnp.int32, (1, 2 * PICKS), 1) & 1) == 0

    def dot_group(g):
        r0 = pl.multiple_of(g * _GRP, _GRP)
        hi, lo = _split_bf16(x_ref[pl.ds(r0, _GRP), :])
        lhs = jnp.concatenate([hi[:, :_HALF], hi[:, _HALF:], lo[:, :_HALF], lo[:, _HALF:]], axis=0)
        tok = lax.broadcasted_iota(jnp.int32, lhs.shape, 0) & (_GRP - 1)
        r = jnp.zeros((4 * _GRP, 2 * PICKS), F32)
        for j in range(_GRP):
            r = r + _dot_nt(jnp.where(tok == j, lhs, jnp.zeros_like(lhs)), _staged_matrix(stg[j]))
        first = r[0:_GRP] + r[2 * _GRP:3 * _GRP]
        second = r[_GRP:2 * _GRP] + r[3 * _GRP:4 * _GRP]
        w_ref[pl.ds(r0, _GRP), :] = jnp.where(even, first, second)

    _staged_pipeline(head_ref, idx_ref, tab_ref, stg, dot_group)

    part = w_ref[...]
    h = part + _swap_pairs(part)
    act = 0.5 * h * (1.0 + lax.erf(h * (2.0 ** -0.5)))
    w_ref[...] = gate_ref[...] * act


def _peer_out_kernel(head_ref, idx_ref, w_ref, x2_ref, tab_ref, y_ref, *stg):
    even = (lax.broadcasted_iota(jnp.int32, (1, 2 * PICKS), 1) & 1) == 0

    def dot_group(g):
        r0 = pl.multiple_of(g * _GRP, _GRP)
        hi, lo = _split_bf16(w_ref[pl.ds(r0, _GRP), :])
        zero = jnp.zeros_like(hi)
        lhs = jnp.concatenate([jnp.where(even, hi, zero), jnp.where(even, zero, hi),
                               jnp.where(even, lo, zero), jnp.where(even, zero, lo)], axis=0)
        tok = lax.broadcasted_iota(jnp.int32, lhs.shape, 0) & (_GRP - 1)
        r = jnp.zeros((4 * _GRP, _HALF), F32)
        for j in range(_GRP):
            r = r + jnp.dot(jnp.where(tok == j, lhs, jnp.zeros_like(lhs)), _staged_matrix(stg[j]),
                            preferred_element_type=F32)
        upd = jnp.concatenate([r[0:_GRP] + r[2 * _GRP:3 * _GRP], r[_GRP:2 * _GRP] + r[3 * _GRP:4 * _GRP]], axis=1)
        y_ref[pl.ds(r0, _GRP), :] = x2_ref[pl.ds(r0, _GRP), :] + upd

    _staged_pipeline(head_ref, idx_ref, tab_ref, stg, dot_group)


def _table_spec(rows):
    return pl.BlockSpec((rows, LANES), lambda i: (0, 0), pipeline_mode=pl.Buffered(1))


def _staging():
    return [pltpu.VMEM((PICKS * _HROWS, LANES), jnp.uint32) for _ in range(_GRP)]


def _offset_specs():
    return [pl.BlockSpec((_GRP * PICKS,), lambda i: (0,), memory_space=pltpu.SMEM),
            pl.BlockSpec((_TT * PICKS,), lambda i: (i,), memory_space=pltpu.SMEM)]


def _shift_offsets(off_flat):
    head = off_flat[:_GRP * PICKS]
    shifted = jnp.concatenate([off_flat[_GRP * PICKS:], jnp.zeros((_GRP * PICKS,), off_flat.dtype)])
    return head, shifted


def _peer_act(off_head, off_shift, hn, tab, gate2):
    t = hn.shape[0]
    return pl.pallas_call(
        _peer_act_kernel,
        out_shape=jax.ShapeDtypeStruct((t, 2 * PICKS), F32),
        grid=(t // _TT,),
        in_specs=_offset_specs() + [
            pl.BlockSpec((_TT, D_MODEL), lambda i: (i, 0)),
            _table_spec(tab.shape[0]),
            pl.BlockSpec((_TT, 2 * PICKS), lambda i: (i, 0)),
        ],
        out_specs=pl.BlockSpec((_TT, 2 * PICKS), lambda i: (i, 0)),
        scratch_shapes=_staging(),
        compiler_params=pltpu.CompilerParams(
            dimension_semantics=("arbitrary",), vmem_limit_bytes=VMEM_LIMIT),
        name="peer_act",
    )(off_head, off_shift, hn, tab, gate2)


def _peer_out(off_head, off_shift, w2, x2, tab):
    t = x2.shape[0]
    return pl.pallas_call(
        _peer_out_kernel,
        out_shape=jax.ShapeDtypeStruct(x2.shape, F32),
        grid=(t // _TT,),
        in_specs=_offset_specs() + [
            pl.BlockSpec((_TT, 2 * PICKS), lambda i: (i, 0)),
            pl.BlockSpec((_TT, D_MODEL), lambda i: (i, 0)),
            _table_spec(tab.shape[0]),
        ],
        out_specs=pl.BlockSpec((_TT, D_MODEL), lambda i: (i, 0)),
        scratch_shapes=_staging(),
        compiler_params=pltpu.CompilerParams(
            dimension_semantics=("arbitrary",), vmem_limit_bytes=VMEM_LIMIT),
        name="peer_out",
    )(off_head, off_shift, w2, x2, tab)


def _rel_bucket(rel):
    nb = NUM_BUCKETS // 2
    max_exact = nb // 2
    n = jnp.abs(rel)
    large = max_exact + (jnp.log(jnp.maximum(n, 1).astype(F32) / max_exact)
                         / math.log(MAX_DISTANCE / max_exact) * (nb - max_exact)).astype(jnp.int32)
    large = jnp.minimum(large, nb - 1)
    return (rel > 0).astype(jnp.int32) * nb + jnp.where(n < max_exact, n, large)


def _bucket_saturation():
    nb = NUM_BUCKETS // 2
    max_exact = nb // 2
    n = np.arange(1, 8192, dtype=np.float64)
    large = max_exact + np.floor(np.log(n / max_exact) / math.log(MAX_DISTANCE / max_exact) * (nb - max_exact))
    return int(n[np.argmax(large >= nb - 1)])


def _toeplitz(v, nq, nk, k0):
    h = v.shape[0]
    width = nq + nk - 1
    w = v[:, k0 - (nq - 1):k0 + nk]
    w = jnp.concatenate([w, jnp.zeros((h, 1), v.dtype)], axis=1)
    rows = jnp.tile(w, (1, nq))[:, :nq * width].reshape(h, nq, width)
    return rows[:, :, nq - 1:nq - 1 + nk]


def _dilated_bias_tiles(bias_a):
    reach = _QT + BAND_HALF
    off = np.arange(-reach, reach + 1)
    tiles = []
    for d in DILATIONS:
        v = bias_a[_rel_bucket(jnp.asarray(off * d, jnp.int32))].astype(F32)
        v = jnp.where(jnp.asarray(np.abs(off) <= BAND_HALF)[:, None], v, NEG_INF)
        tiles.append(_toeplitz(v.T, _QT, _KT, reach - BAND_HALF))
    return jnp.stack(tiles, 0)


def _diff_bias_tiles(bias_b):
    assert (_DB - 1) * _DQ + 1 >= _bucket_saturation() + _DQ // 2
    reach = (_DB + 1) * _DQ
    rel = np.arange(-reach, reach + 1)
    v = bias_b[_rel_bucket(jnp.asarray(rel, jnp.int32))].astype(F32).T
    tiles =[_toeplitz(v, _DQ, _DQ, reach + delta * _DQ) for delta in range(-_DB, _DB + 1)]
    return jnp.stack(tiles, 1)


def _pack_table(w):
    e = w.shape[0]
    bits = lax.bitcast_convert_type(w.astype(BF16), jnp.uint16).astype(jnp.uint32)
    packed = bits[:, :_HALF] | (bits[:, _HALF:] << 16)
    return packed.reshape(e * _HROWS, LANES)


def _tile_gain(g, scale=1.0):
    return jnp.tile(g.astype(F32) * scale, SEG // HEAD_DIM).reshape(1, SEG)


def kernel(x_prompt, x_sample, rel_bias, attn_norm_g, w_in, q_norm_a, k_norm_a, q_norm_b, k_norm_b,
           lambda_q1, lambda_k1, lambda_q2, lambda_k2, diff_norm_g, w_out, ffn_norm_g,
           peer_w_q, peer_sub_keys, peer_u, peer_v):
    gsum = jnp.asarray(np.kron(np.eye(SEG // HEAD_DIM), np.ones((HEAD_DIM, HEAD_DIM))), BF16)
    gains = jnp.stack([_tile_gain(q_norm_a[0], ATTN_SCALE), _tile_gain(k_norm_a[0]),
                       _tile_gain(q_norm_b[0], ATTN_SCALE), _tile_gain(k_norm_b[0])], 0)
    attn_g = attn_norm_g[0].reshape(1, D_MODEL).astype(F32)
    ffn_g = ffn_norm_g[0].reshape(1, D_MODEL).astype(F32)
    w_in_bf, w_out_bf, w_q_bf = w_in[0].astype(BF16), w_out[0].astype(BF16), peer_w_q[0].astype(BF16)
    sk_bf = peer_sub_keys[0].astype(BF16)
    bias_a = _dilated_bias_tiles(rel_bias[:, :N_HEADS_A])
    bias_b = _diff_bias_tiles(rel_bias[:, N_HEADS_A:])
    lam = (jnp.exp(jnp.sum(lambda_q1[0].astype(F32) * lambda_k1[0].astype(F32)))
           - jnp.exp(jnp.sum(lambda_q2[0].astype(F32) * lambda_k2[0].astype(F32)))
           + LAMBDA_INIT).reshape(1)
    dgain = diff_norm_g[0].astype(F32).reshape(1, LANES)
    tab_u, tab_v = _pack_table(peer_u[0]), _pack_table(peer_v[0])
    tm = 512
    cflat_b = jnp.asarray(np.broadcast_to(_cand_layout()[:, None], (_CAND_ROWS, tm)).copy())

    def encode(x3):
        batch, seq, _ = x3.shape
        t = batch * seq
        x = x3.reshape(t, D_MODEL)
        proj3 = _in_proj(x, attn_g, w_in_bf, gsum, gains).reshape(batch, seq, IN_WIDTH)
        oa = _dilated_attention(proj3, bias_a, batch, seq).reshape(t, SEG)
        ob = _diff_attention(lam, proj3, bias_b, dgain, batch, seq).reshape(t, SEG)
        x2, hn, qp = _out_proj(x, oa, ob, w_out_bf, ffn_g, w_q_bf)
        eidx_t, gate_t = _topk(qp, sk_bf, cflat_b, tm)
        off_flat = (eidx_t.T * _HROWS).reshape(t * PICKS)
        gate2 = jnp.repeat(gate_t.T, 2, axis=1)
        off_head, off_shift = _shift_offsets(off_flat)
        w2 = _peer_act(off_head, off_shift, hn, tab_u, gate2)
        return _peer_out(off_head, off_shift, w2, x2, tab_v).reshape(batch, seq, D_MODEL)

    return (encode(x_prompt), encode(x_sample))
```
